```python
import math
import jax, jax.numpy as jnp
from jax import lax
import numpy as np

D_MODEL = 2048
BATCH = 1
SEQ = 8192
DEPTH = 2

S5_WIDTH = D_MODEL // 2
S5_GROUP = 16
S5_GROUPS = S5_WIDTH // S5_GROUP
S5_STATE = 64
LRU_WIDTH = D_MODEL // 2
LRU_BLOCKS = 8
LRU_BLOCK = LRU_WIDTH // LRU_BLOCKS
LRU_C = 8.0
CONV_K = 4
GDN_HEADS = 8
GDN_DK = 128
GDN_DV = 128
GDN_QK = GDN_HEADS * GDN_DK
GDN_V = GDN_HEADS * GDN_DV
GDN_CHUNK = 64
N_BRANCH = 3
BRANCH_WIDTH = D_MODEL // 2
MEM_LEN = 256
XA_HEADS = 4
XA_HEAD_DIM = D_MODEL // XA_HEADS
PEER_HEADS = 8
PEER_NKEYS = 128
PEER_EXPERTS = PEER_NKEYS * PEER_NKEYS
PEER_DKEY = 256
PEER_TOPK = 16
PEER_BLOCK = 128
DN_ALPHA = (2 * DEPTH) ** 0.25
DN_BETA = (8 * DEPTH) ** -0.25
LN_EPS = 1e-5
IN_SPLITS = (S5_WIDTH, LRU_WIDTH, LRU_WIDTH, 2 * GDN_QK + GDN_V, GDN_V, GDN_HEADS, GDN_HEADS, N_BRANCH * D_MODEL)
IN_WIDTH = sum(IN_SPLITS)
IN_OFFSETS = tuple(sum(IN_SPLITS[:i + 1]) for i in range(len(IN_SPLITS) - 1))

kernel_name = 'hybrid_s5_rglru_gdn_peer_deepnorm'


def layer_norm(x, g, b):
    xf = x.astype(jnp.float32)
    mu = jnp.mean(xf, axis=-1, keepdims=True)
    var = jnp.mean(jnp.square(xf - mu), axis=-1, keepdims=True)
    return (xf - mu) * lax.rsqrt(var + LN_EPS) * g.astype(jnp.float32) + b.astype(jnp.float32)


def l2_normalize(t):
    return t * lax.rsqrt(jnp.sum(t * t, axis=-1, keepdims=True) + 1e-6)


def causal_dwconv(x, w):
    return lax.conv_general_dilated(
        x, w[:, None, :].astype(x.dtype), window_strides=(1,),
        padding=[(w.shape[0] - 1, 0)], dimension_numbers=('NWC', 'WIO', 'NWC'),
        feature_group_count=x.shape[-1])


def linear_scan(a, b):
    def combine(l, r):
        return l[0] * r[0], r[0] * l[1] + r[1]
    return lax.associative_scan(combine, (a, b), axis=1)[1]


def complex_linear_scan(a_re, a_im, b_re, b_im):
    def combine(l, r):
        ar1, ai1, br1, bi1 = l
        ar2, ai2, br2, bi2 = r
        return (ar2 * ar1 - ai2 * ai1, ar2 * ai1 + ai2 * ar1,
                ar2 * br1 - ai2 * bi1 + br2, ar2 * bi1 + ai2 * br1 + bi2)
    out = lax.associative_scan(combine, (a_re, a_im, b_re, b_im), axis=1)
    return out[2], out[3]


def s5_mixer(u, a_re, a_im, log_dt, b_re, b_im, c_re, c_im, d_skip, w_glu, b_glu):
    f32 = jnp.float32
    bsz, seq, _ = u.shape
    uf = u.astype(f32).reshape(bsz, seq, S5_GROUPS, S5_GROUP)
    lam_re, lam_im = a_re.astype(f32), a_im.astype(f32)
    dt = jnp.exp(log_dt.astype(f32))[:, None]
    mag = jnp.exp(lam_re * dt)
    abar_re, abar_im = mag * jnp.cos(lam_im * dt), mag * jnp.sin(lam_im * dt)
    den = lam_re * lam_re + lam_im * lam_im
    num_re = abar_re - 1.0
    f_re = (num_re * lam_re + abar_im * lam_im) / den
    f_im = (abar_im * lam_re - num_re * lam_im) / den
    bre, bim = b_re.astype(f32), b_im.astype(f32)
    bbar_re = f_re[..., None] * bre - f_im[..., None] * bim
    bbar_im = f_re[..., None] * bim + f_im[..., None] * bre
    bu_re = jnp.einsum('bsgp,gnp->bsgn', uf, bbar_re)
    bu_im = jnp.einsum('bsgp,gnp->bsgn', uf, bbar_im)
    s_re, s_im = complex_linear_scan(jnp.broadcast_to(abar_re, bu_re.shape),
                                     jnp.broadcast_to(abar_im, bu_im.shape), bu_re, bu_im)
    y = (jnp.einsum('bsgn,gpn->bsgp', s_re, c_re.astype(f32))
         - jnp.einsum('bsgn,gpn->bsgp', s_im, c_im.astype(f32))
         + d_skip.astype(f32) * uf)
    y = jax.nn.gelu(y.reshape(bsz, seq, S5_WIDTH))
    return y * jax.nn.sigmoid(y @ w_glu.astype(f32) + b_glu.astype(f32))


def rglru_mixer(x_in, gate_in, conv_w, conv_b, w_a, b_a, w_x, b_x, lam):
    f32 = jnp.float32
    bsz, seq, _ = x_in.shape
    xc = (causal_dwconv(x_in, conv_w) + conv_b).astype(f32)
    xh = xc.reshape(bsz, seq, LRU_BLOCKS, LRU_BLOCK)
    r = jax.nn.sigmoid(jnp.einsum('bshi,hij->bshj', xh, w_a.astype(f32)) + b_a.astype(f32))
    i = jax.nn.sigmoid(jnp.einsum('bshi,hij->bshj', xh, w_x.astype(f32)) + b_x.astype(f32))
    r = r.reshape(bsz, seq, LRU_WIDTH)
    i = i.reshape(bsz, seq, LRU_WIDTH)
    log_a = -LRU_C * r * jax.nn.softplus(-lam.astype(f32))
    a = jnp.exp(log_a)
    b = jnp.sqrt(-jnp.expm1(2.0 * log_a)) * (i * xc)
    h = linear_scan(a, b)
    return h * jax.nn.gelu(gate_in.astype(f32))


def chunk_gated_delta_rule(q, k, v, g, beta):
    f32 = jnp.float32
    bsz, nh, seq, dk = q.shape
    dv = v.shape[-1]
    c = GDN_CHUNK
    n = seq // c
    q = q.reshape(bsz, nh, n, c, dk)
    k = k.reshape(bsz, nh, n, c, dk)
    v = v.reshape(bsz, nh, n, c, dv)
    beta = beta.reshape(bsz, nh, n, c)
    g = jnp.cumsum(g.reshape(bsz, nh, n, c), axis=-1)
    causal = jnp.tril(jnp.ones((c, c), dtype=bool))
    strict = jnp.tril(jnp.ones((c, c), dtype=bool), k=-1)
    decay = jnp.exp(jnp.where(causal, g[..., :, None] - g[..., None, :], -jnp.inf))
    k_beta = k * beta[..., None]
    v_beta = v * beta[..., None]
    a_mat = jnp.where(strict, jnp.einsum('bhnid,bhnjd->bhnij', k_beta, k) * decay, 0.0)
    eye = jnp.eye(c, dtype=f32)
    t_inv = lax.linalg.triangular_solve(a_mat + eye, jnp.broadcast_to(eye, a_mat.shape),
                                        left_side=True, lower=True, unit_diagonal=True)
    w = jnp.einsum('bhnij,bhnjd->bhnid', t_inv, k_beta * jnp.exp(g)[..., None])
    u = jnp.einsum('bhnij,bhnjd->bhnid', t_inv, v_beta)
    intra = jnp.where(causal, jnp.einsum('bhnid,bhnjd->bhnij', q, k) * decay, 0.0)
    chunk_first = lambda t: jnp.moveaxis(t, 2, 0)

    def step(state, inp):
        q_c, k_c, u_c, w_c, g_c, a_c = inp
        v_new = u_c - jnp.einsum('bhck,bhkv->bhcv', w_c, state)
        o_c = (jnp.einsum('bhck,bhkv->bhcv', q_c * jnp.exp(g_c)[..., None], state)
               + jnp.einsum('bhcj,bhjv->bhcv', a_c, v_new))
        g_last = g_c[..., -1:]
        k_dec = k_c * jnp.exp(g_last - g_c)[..., None]
        state = state * jnp.exp(g_last)[..., None] + jnp.einsum('bhck,bhcv->bhkv', k_dec, v_new)
        return state, o_c

    state0 = jnp.zeros((bsz, nh, dk, dv), f32)
    _, o = lax.scan(step, state0, (chunk_first(q), chunk_first(k), chunk_first(u),
                                   chunk_first(w), chunk_first(g), chunk_first(intra)))
    return jnp.moveaxis(o, 0, 2).reshape(bsz, nh, seq, dv)


def gated_deltanet_mixer(qkv, o_gate, beta_in, a_in, conv_w, a_log, dt_bias, norm_g):
    f32 = jnp.float32
    bsz, seq, _ = qkv.shape
    qkv = jax.nn.silu(causal_dwconv(qkv, conv_w).astype(f32))
    q, k, v = jnp.split(qkv, (GDN_QK, 2 * GDN_QK), axis=-1)
    heads = lambda t, d: t.reshape(bsz, seq, GDN_HEADS, d).transpose(0, 2, 1, 3)
    q = l2_normalize(heads(q, GDN_DK)) * GDN_DK ** -0.5
    k = l2_normalize(heads(k, GDN_DK))
    v = heads(v, GDN_DV)
    beta = jax.nn.sigmoid(beta_in.astype(f32)).transpose(0, 2, 1)
    g = (-jnp.exp(a_log.astype(f32)) * jax.nn.softplus(a_in.astype(f32) + dt_bias.astype(f32))).transpose(0, 2, 1)
    o = chunk_gated_delta_rule(q, k, v, g, beta).transpose(0, 2, 1, 3)
    o = o * lax.rsqrt(jnp.mean(o * o, axis=-1, keepdims=True) + 1e-6) * norm_g.astype(f32)
    o = o * jax.nn.silu(o_gate.astype(f32).reshape(bsz, seq, GDN_HEADS, GDN_DV))
    return o.reshape(bsz, seq, GDN_V)


def hybrid_mixer(x, w_in, s5_a_re, s5_a_im, s5_log_dt, s5_b_re, s5_b_im, s5_c_re, s5_c_im, s5_d,
                 s5_w_glu, s5_b_glu, lru_conv_w, lru_conv_b, lru_w_a, lru_b_a, lru_w_x, lru_b_x,
                 lru_lambda, gdn_conv_w, gdn_a_log, gdn_dt_bias, gdn_norm_g, w_branch, w_out):
    f32 = jnp.float32
    bsz, seq, _ = x.shape
    proj = x @ w_in
    u_s5, x_lru, g_lru, qkv, o_gate, beta_in, a_in, gate_logits = jnp.split(proj, IN_OFFSETS, axis=-1)
    y_a = s5_mixer(u_s5, s5_a_re, s5_a_im, s5_log_dt, s5_b_re, s5_b_im, s5_c_re, s5_c_im, s5_d,
                   s5_w_glu, s5_b_glu)
    y_b = rglru_mixer(x_lru, g_lru, lru_conv_w, lru_conv_b, lru_w_a, lru_b_a, lru_w_x, lru_b_x, lru_lambda)
    y_c = gated_deltanet_mixer(qkv, o_gate, beta_in, a_in, gdn_conv_w, gdn_a_log, gdn_dt_bias, gdn_norm_g)
    branches = jnp.stack([y_a, y_b, y_c], axis=2)
    branch_d = jnp.einsum('bsnc,ncd->bsnd', branches, w_branch.astype(f32))
    gates = jax.nn.sigmoid(gate_logits.astype(f32)).reshape(bsz, seq, N_BRANCH, D_MODEL)
    merged = jnp.sum(gates * branch_d, axis=2)
    return merged @ w_out.astype(f32)


def memory_cross_attention(x, mem, wq, wk, wv, wo):
    bsz, seq, _ = x.shape
    m = mem.shape[1]
    q = (x @ wq).reshape(bsz, seq, XA_HEADS, XA_HEAD_DIM)
    k = (mem @ wk).reshape(bsz, m, XA_HEADS, XA_HEAD_DIM)
    v = (mem @ wv).reshape(bsz, m, XA_HEADS, XA_HEAD_DIM)
    s = jnp.einsum('bshd,bmhd->bhsm', q, k).astype(jnp.float32) * XA_HEAD_DIM ** -0.5
    p = jax.nn.softmax(s, axis=-1).astype(v.dtype)
    o = jnp.einsum('bhsm,bmhd->bshd', p, v).reshape(bsz, seq, D_MODEL)
    return o @ wo


def peer_ffn(x, wq, sub_keys, u_tab, v_tab):
    f32 = jnp.float32
    bsz, seq, d = x.shape
    q = (x @ wq).astype(f32).reshape(bsz, seq, PEER_HEADS, 2, PEER_DKEY // 2)
    s = jnp.einsum('bshpd,hpkd->bshpk', q, sub_keys.astype(f32))
    top_s, top_i = lax.top_k(s, PEER_TOPK)
    cand_s = (top_s[..., 0, :, None] + top_s[..., 1, None, :]).reshape(bsz, seq, PEER_HEADS, PEER_TOPK * PEER_TOPK)
    cand_i = (top_i[..., 0, :, None] * PEER_NKEYS + top_i[..., 1, None, :]).reshape(bsz, seq, PEER_HEADS, PEER_TOPK * PEER_TOPK)
    best_s, best_pos = lax.top_k(cand_s, PEER_TOPK)
    idx = jnp.take_along_axis(cand_i, best_pos, axis=-1)
    gate = jax.nn.softmax(best_s, axis=-1)
    nb = (bsz * seq) // PEER_BLOCK
    xb = x.reshape(nb, PEER_BLOCK, d)
    ib = idx.reshape(nb, PEER_BLOCK, PEER_HEADS, PEER_TOPK)
    gb = gate.reshape(nb, PEER_BLOCK, PEER_HEADS, PEER_TOPK)

    def block(args):
        xt, it, gt = args
        u = jnp.take(u_tab, it, axis=0)
        act = jax.nn.gelu(jnp.einsum('thkd,td->thk', u, xt).astype(f32)) * gt
        vv = jnp.take(v_tab, it, axis=0)
        return jnp.einsum('thk,thkd->td', act, vv.astype(f32))

    return lax.map(block, (xb, ib, gb)).reshape(bsz, seq, d)


def setup_inputs(seed: int = 0) -> dict:
    key = jax.random.key(seed)
    keys = iter(jax.random.split(key, 64))
    f32 = jnp.float32
    L = DEPTH

    def normal(shape, scale):
        return jax.random.normal(next(keys), shape, f32) * scale

    def uniform(shape, lo, hi):
        return jax.random.uniform(next(keys), shape, f32, lo, hi)

    def gain(shape):
        return 1.0 + normal(shape, 0.02)

    n_idx = jnp.arange(S5_STATE, dtype=f32)
    lru_a1 = uniform((L, LRU_WIDTH), 0.9, 0.999) ** (1.0 / LRU_C)
    gdn_dt = jnp.exp(uniform((L, GDN_HEADS), math.log(1e-3), math.log(1e-1)))
    return {
        'x': normal((BATCH, SEQ, D_MODEL), 1.0),
        'mem': normal((BATCH, MEM_LEN, D_MODEL), 1.0),
        'ln_mix_g': gain((L, D_MODEL)),
        'ln_mix_b': normal((L, D_MODEL), 0.02),
        'w_in': normal((L, D_MODEL, IN_WIDTH), D_MODEL ** -0.5),
        's5_a_re': -0.5 * jnp.exp(normal((L, S5_GROUPS, S5_STATE), 0.05)),
        's5_a_im': math.pi * n_idx + normal((L, S5_GROUPS, S5_STATE), 0.05),
        's5_log_dt': uniform((L, S5_GROUPS), math.log(1e-3), math.log(1e-1)),
        's5_b_re': normal((L, S5_GROUPS, S5_STATE, S5_GROUP), (2 * S5_GROUP) ** -0.5),
        's5_b_im': normal((L, S5_GROUPS, S5_STATE, S5_GROUP), (2 * S5_GROUP) ** -0.5),
        's5_c_re': normal((L, S5_GROUPS, S5_GROUP, S5_STATE), S5_STATE ** -0.5),
        's5_c_im': normal((L, S5_GROUPS, S5_GROUP, S5_STATE), S5_STATE ** -0.5),
        's5_d': normal((L, S5_GROUPS, S5_GROUP), 1.0),
        's5_w_glu': normal((L, S5_WIDTH, S5_WIDTH), S5_WIDTH ** -0.5),
        's5_b_glu': normal((L, S5_WIDTH), 0.02),
        'lru_conv_w': normal((L, CONV_K, LRU_WIDTH), CONV_K ** -0.5),
        'lru_conv_b': normal((L, LRU_WIDTH), 0.02),
        'lru_w_a': normal((L, LRU_BLOCKS, LRU_BLOCK, LRU_BLOCK), LRU_BLOCK ** -0.5),
        'lru_b_a': normal((L, LRU_BLOCKS, LRU_BLOCK), 0.02),
        'lru_w_x': normal((L, LRU_BLOCKS, LRU_BLOCK, LRU_BLOCK), LRU_BLOCK ** -0.5),
        'lru_b_x': normal((L, LRU_BLOCKS, LRU_BLOCK), 0.02),
        'lru_lambda': jnp.log(lru_a1) - jnp.log1p(-lru_a1),
        'gdn_conv_w': normal((L, CONV_K, 2 * GDN_QK + GDN_V), CONV_K ** -0.5),
        'gdn_a_log': jnp.log(uniform((L, GDN_HEADS), 1.0, 16.0)),
        'gdn_dt_bias': gdn_dt + jnp.log(-jnp.expm1(-gdn_dt)),
        'gdn_norm_g': gain((L, GDN_DV)),
        'w_branch': normal((L, N_BRANCH, BRANCH_WIDTH, D_MODEL), BRANCH_WIDTH ** -0.5),
        'w_out': normal((L, D_MODEL, D_MODEL), DN_BETA * D_MODEL ** -0.5),
        'ln_xa_g': gain((L, D_MODEL)),
        'ln_xa_b': normal((L, D_MODEL), 0.02),
        'xa_wq': normal((L, D_MODEL, D_MODEL), D_MODEL ** -0.5),
        'xa_wk': normal((L, D_MODEL, D_MODEL), D_MODEL ** -0.5),
        'xa_wv': normal((L, D_MODEL, D_MODEL), DN_BETA * D_MODEL ** -0.5),
        'xa_wo': normal((L, D_MODEL, D_MODEL), DN_BETA * D_MODEL ** -0.5),
        'ln_ffn_g': gain((L, D_MODEL)),
        'ln_ffn_b': normal((L, D_MODEL), 0.02),
        'peer_wq': normal((L, D_MODEL, PEER_HEADS * PEER_DKEY), D_MODEL ** -0.5),
        'peer_keys': normal((L, PEER_HEADS, 2, PEER_NKEYS, PEER_DKEY // 2), (PEER_DKEY // 2) ** -0.5),
        'peer_u': normal((L, PEER_EXPERTS, D_MODEL), D_MODEL ** -0.5),
        'peer_v': normal((L, PEER_EXPERTS, D_MODEL), DN_BETA * PEER_HEADS ** -0.5),
    }


def reference(x, mem, ln_mix_g, ln_mix_b, w_in, s5_a_re, s5_a_im, s5_log_dt, s5_b_re, s5_b_im,
              s5_c_re, s5_c_im, s5_d, s5_w_glu, s5_b_glu, lru_conv_w, lru_conv_b, lru_w_a, lru_b_a,
              lru_w_x, lru_b_x, lru_lambda, gdn_conv_w, gdn_a_log, gdn_dt_bias, gdn_norm_g, w_branch,
              w_out, ln_xa_g, ln_xa_b, xa_wq, xa_wk, xa_wv, xa_wo, ln_ffn_g, ln_ffn_b, peer_wq,
              peer_keys, peer_u, peer_v):
    for l in range(DEPTH):
        h = hybrid_mixer(x, w_in[l], s5_a_re[l], s5_a_im[l], s5_log_dt[l], s5_b_re[l], s5_b_im[l],
                         s5_c_re[l], s5_c_im[l], s5_d[l], s5_w_glu[l], s5_b_glu[l], lru_conv_w[l],
                         lru_conv_b[l], lru_w_a[l], lru_b_a[l], lru_w_x[l], lru_b_x[l], lru_lambda[l],
                         gdn_conv_w[l], gdn_a_log[l], gdn_dt_bias[l], gdn_norm_g[l], w_branch[l], w_out[l])
        x = layer_norm(DN_ALPHA * x + h, ln_mix_g[l], ln_mix_b[l])
        h = memory_cross_attention(x, mem, xa_wq[l], xa_wk[l], xa_wv[l], xa_wo[l])
        x = layer_norm(DN_ALPHA * x + h, ln_xa_g[l], ln_xa_b[l])
        h = peer_ffn(x, peer_wq[l], peer_keys[l], peer_u[l], peer_v[l])
        x = layer_norm(DN_ALPHA * x + h, ln_ffn_g[l], ln_ffn_b[l])
    return x
```

```python
import functools
import math

import jax
import jax.numpy as jnp
from jax import lax
from jax.experimental import pallas as pl
from jax.experimental.pallas import tpu as pltpu

F32 = jnp.float32
BF16 = jnp.bfloat16

S5_GROUP = 16
S5_STATE = 64
S5_CHUNK = 16
LRU_BLOCKS = 8
LRU_C = 8.0
CONV_K = 4
GDN_HEADS = 8
GDN_DK = 128
GDN_DV = 128
GDN_CHUNK = 64
N_BRANCH = 3
XA_HEADS = 4
PEER_HEADS = 8
PEER_NKEYS = 128
PEER_TOPK = 16
DEPTH = 2
DN_ALPHA = (2 * DEPTH) ** 0.25
LN_EPS = 1e-5
NEG_BIG = -1e30
POS_BIG = 1e30
VMEM_LIMIT = 56 * 1024 * 1024


def _cparams(*sem):
    return pltpu.CompilerParams(dimension_semantics=sem, vmem_limit_bytes=VMEM_LIMIT)


def _gelu(x):
    return 0.5 * x * (1.0 + jnp.tanh(math.sqrt(2.0 / math.pi) * (x + 0.044715 * (x * x * x))))


def _sigmoid(x):
    return 1.0 / (1.0 + jnp.exp(-x))


def _softplus(x):
    return jnp.maximum(x, 0.0) + jnp.log1p(jnp.exp(-jnp.abs(x)))


def _layer_norm(y, g, b):
    mu = jnp.mean(y, axis=-1, keepdims=True)
    d = y - mu
    var = jnp.mean(d * d, axis=-1, keepdims=True)
    return d * lax.rsqrt(var + LN_EPS) * g + b


def _mm_kernel(a_ref, b_ref, o_ref):
    o_ref[...] = jnp.dot(a_ref[...], b_ref[...], preferred_element_type=F32).astype(o_ref.dtype)


def mm(a, b, out_dtype, tm, tn, name):
    m, k = a.shape
    n = b.shape[1]
    return pl.pallas_call(
        _mm_kernel,
        grid=(m // tm, n // tn),
        in_specs=[pl.BlockSpec((tm, k), lambda i, j: (i, 0)),
                  pl.BlockSpec((k, tn), lambda i, j: (0, j))],
        out_specs=pl.BlockSpec((tm, tn), lambda i, j: (i, j)),
        out_shape=jax.ShapeDtypeStruct((m, n), out_dtype),
        compiler_params=_cparams("parallel", "parallel"),
        name=name,
    )(a, b)


def _mm_res_ln_kernel(a_ref, w_ref, x_ref, g_ref, b_ref, o_ref, ob_ref):
    h = jnp.dot(a_ref[...], w_ref[...], preferred_element_type=F32)
    y = _layer_norm(DN_ALPHA * x_ref[...] + h, g_ref[...], b_ref[...])
    o_ref[...] = y
    ob_ref[...] = y.astype(BF16)


def mm_res_ln(a, w, x, g, b, tm, name):
    m, k = a.shape
    n = w.shape[1]
    row = lambda i: (i, 0)
    fixed = lambda i: (0, 0)
    return pl.pallas_call(
        _mm_res_ln_kernel,
        grid=(m // tm,),
        in_specs=[pl.BlockSpec((tm, k), row), pl.BlockSpec((k, n), fixed), pl.BlockSpec((tm, n), row),
                  pl.BlockSpec((1, n), fixed), pl.BlockSpec((1, n), fixed)],
        out_specs=[pl.BlockSpec((tm, n), row), pl.BlockSpec((tm, n), row)],
        out_shape=[jax.ShapeDtypeStruct((m, n), F32), jax.ShapeDtypeStruct((m, n), BF16)],
        compiler_params=_cparams("parallel"),
        name=name,
    )(a, w, x, g.reshape(1, n), b.reshape(1, n))


def _res_ln_kernel(h_ref, x_ref, g_ref, b_ref, o_ref, ob_ref):
    y = _layer_norm(DN_ALPHA * x_ref[...] + h_ref[...], g_ref[...], b_ref[...])
    o_ref[...] = y
    ob_ref[...] = y.astype(BF16)


def res_ln(h, x, g, b, tm, name):
    m, n = x.shape
    row = lambda i: (i, 0)
    fixed = lambda i: (0, 0)
    return pl.pallas_call(
        _res_ln_kernel,
        grid=(m // tm,),
        in_specs=[pl.BlockSpec((tm, n), row), pl.BlockSpec((tm, n), row),
                  pl.BlockSpec((1, n), fixed), pl.BlockSpec((1, n), fixed)],
        out_specs=[pl.BlockSpec((tm, n), row), pl.BlockSpec((tm, n), row)],
        out_shape=[jax.ShapeDtypeStruct((m, n), F32), jax.ShapeDtypeStruct((m, n), BF16)],
        compiler_params=_cparams("parallel"),
        name=name,
    )(h, x, g.reshape(1, n), b.reshape(1, n))


def s5_prepare(a_re, a_im, log_dt, b_re, b_im, c_re, c_im, d_skip):
    hi = lax.Precision.HIGHEST
    L = S5_CHUNK
    lam_re, lam_im = a_re.astype(F32), a_im.astype(F32)
    dt = jnp.exp(log_dt.astype(F32))[:, None]
    mag = jnp.exp(lam_re * dt)
    abar_re, abar_im = mag * jnp.cos(lam_im * dt), mag * jnp.sin(lam_im * dt)
    den = lam_re * lam_re + lam_im * lam_im
    num_re = abar_re - 1.0
    f_re = (num_re * lam_re + abar_im * lam_im) / den
    f_im = (abar_im * lam_re - num_re * lam_im) / den
    bre, bim = b_re.astype(F32), b_im.astype(F32)
    bbar_re = f_re[..., None] * bre - f_im[..., None] * bim
    bbar_im = f_re[..., None] * bim + f_im[..., None] * bre
    steps = jnp.arange(L + 1, dtype=F32)[:, None, None]
    pw_mag = jnp.exp(lam_re * dt * steps)
    pw_re = pw_mag * jnp.cos(lam_im * dt * steps)
    pw_im = pw_mag * jnp.sin(lam_im * dt * steps)
    cre, cim = c_re.astype(F32), c_im.astype(F32)
    ca_re = cre[None] * pw_re[:, :, None, :] - cim[None] * pw_im[:, :, None, :]
    ca_im = cre[None] * pw_im[:, :, None, :] + cim[None] * pw_re[:, :, None, :]
    kern = (jnp.einsum('dgpn,gnq->dgpq', ca_re[:L], bbar_re, precision=hi)
            - jnp.einsum('dgpn,gnq->dgpq', ca_im[:L], bbar_im, precision=hi))
    kern = kern.at[0].add(d_skip.astype(F32)[:, :, None] * jnp.eye(S5_GROUP, dtype=F32)[None])
    lag = jnp.arange(L)[None, :] - jnp.arange(L)[:, None]
    t5 = jnp.where((lag >= 0)[:, :, None, None, None], kern[jnp.clip(lag, 0, L - 1)], 0.0)
    g = a_re.shape[0]
    t_mat = t5.transpose(2, 0, 4, 1, 3).reshape(g, L * S5_GROUP, L * S5_GROUP)
    rev_re, rev_im = pw_re[L - 1::-1][:L], pw_im[L - 1::-1][:L]
    bc_re = rev_re[..., None] * bbar_re[None] - rev_im[..., None] * bbar_im[None]
    bc_im = rev_re[..., None] * bbar_im[None] + rev_im[..., None] * bbar_re[None]
    bc_mat = jnp.concatenate([bc_re.transpose(1, 0, 3, 2), bc_im.transpose(1, 0, 3, 2)],
                             axis=-1).reshape(g, L * S5_GROUP, 2 * S5_STATE)
    cc_re = ca_re[1:].transpose(1, 3, 0, 2)
    cc_im = -ca_im[1:].transpose(1, 3, 0, 2)
    cc_mat = jnp.concatenate([cc_re, cc_im], axis=1).reshape(g, 2 * S5_STATE, L * S5_GROUP)
    a1 = jnp.concatenate([pw_re[L], pw_re[L]], axis=-1)
    a2 = jnp.concatenate([-pw_im[L], pw_im[L]], axis=-1)
    return t_mat.astype(BF16), bc_mat.astype(BF16), cc_mat.astype(BF16), a1, a2


def _s5_z_kernel(u_ref, bc_ref, z_ref):
    z_ref[...] = jnp.dot(u_ref[0], bc_ref[0], preferred_element_type=F32)


def _s5_scan_kernel(z_ref, a1_ref, a2_ref, s_ref, carry_ref):
    @pl.when(pl.program_id(0) == 0)
    def _():
        carry_ref[...] = jnp.zeros_like(carry_ref)

    a1 = a1_ref[...]
    a2 = a2_ref[...]

    def body(c, state):
        s_ref[c] = state
        swapped = pltpu.roll(state, S5_STATE, axis=1)
        return a1 * state + a2 * swapped + z_ref[c]

    carry_ref[...] = lax.fori_loop(0, z_ref.shape[0], body, carry_ref[...])


def _s5_out_kernel(u_ref, t_ref, s_ref, cc_ref, y_ref):
    y = jnp.dot(u_ref[0], t_ref[0], preferred_element_type=F32)
    y = y + jnp.dot(s_ref[...].astype(BF16), cc_ref[0], preferred_element_type=F32)
    y_ref[0] = _gelu(y)


def s5_branch(u_r, t_mat, bc_mat, cc_mat, a1, a2):
    g, c, w = u_r.shape
    ns = 2 * S5_STATE
    z = pl.pallas_call(
        _s5_z_kernel,
        grid=(g,),
        in_specs=[pl.BlockSpec((1, c, w), lambda i: (i, 0, 0)), pl.BlockSpec((1, w, ns), lambda i: (i, 0, 0))],
        out_specs=pl.BlockSpec((c, ns), lambda i: (0, i)),
        out_shape=jax.ShapeDtypeStruct((c, g * ns), F32),
        compiler_params=_cparams("parallel"),
        name="s5_chunk_inputs",
    )(u_r, bc_mat)
    cb = min(64, c)
    s_prev = pl.pallas_call(
        _s5_scan_kernel,
        grid=(c // cb,),
        in_specs=[pl.BlockSpec((cb, g, ns), lambda i: (i, 0, 0)),
                  pl.BlockSpec((g, ns), lambda i: (0, 0)), pl.BlockSpec((g, ns), lambda i: (0, 0))],
        out_specs=pl.BlockSpec((cb, g, ns), lambda i: (i, 0, 0)),
        out_shape=jax.ShapeDtypeStruct((c, g, ns), F32),
        scratch_shapes=[pltpu.VMEM((g, ns), F32)],
        compiler_params=_cparams("arbitrary"),
        name="s5_chunk_scan",
    )(z.reshape(c, g, ns), a1, a2)
    return pl.pallas_call(
        _s5_out_kernel,
        grid=(g,),
        in_specs=[pl.BlockSpec((1, c, w), lambda i: (i, 0, 0)), pl.BlockSpec((1, w, w), lambda i: (i, 0, 0)),
                  pl.BlockSpec((c, ns), lambda i: (0, i)), pl.BlockSpec((1, ns, w), lambda i: (i, 0, 0))],
        out_specs=pl.BlockSpec((1, c, w), lambda i: (i, 0, 0)),
        out_shape=jax.ShapeDtypeStruct((g, c, w), F32),
        compiler_params=_cparams("parallel"),
        name="s5_outputs",
    )(u_r, t_mat, s_prev.reshape(c, g * ns), cc_mat)


def _glu_kernel(y_ref, w_ref, b_ref, o_ref):
    y = y_ref[...]
    gate = jnp.dot(y.astype(BF16), w_ref[...], preferred_element_type=F32) + b_ref[...]
    o_ref[...] = (y * _sigmoid(gate)).astype(o_ref.dtype)


def glu(y, w, b, tm):
    m, n = y.shape
    return pl.pallas_call(
        _glu_kernel,
        grid=(m // tm,),
        in_specs=[pl.BlockSpec((tm, n), lambda i: (i, 0)), pl.BlockSpec((n, n), lambda i: (0, 0)),
                  pl.BlockSpec((1, n), lambda i: (0, 0))],
        out_specs=pl.BlockSpec((tm, n), lambda i: (i, 0)),
        out_shape=jax.ShapeDtypeStruct((m, n), BF16),
        compiler_params=_cparams("parallel"),
        name="s5_glu",
    )(y, w, b.reshape(1, n))


def _lru_kernel(x_ref, gate_ref, cw_ref, cb_ref, wa_ref, ba_ref, wx_ref, bx_ref, lam_ref, o_ref,
                tail_ref, h_ref, a_ref, b_ref):
    tb, width = x_ref.shape
    blk = width // LRU_BLOCKS

    @pl.when(pl.program_id(0) == 0)
    def _():
        tail_ref[...] = jnp.zeros_like(tail_ref)
        h_ref[...] = jnp.zeros_like(h_ref)

    x = x_ref[...]
    ext = jnp.concatenate([tail_ref[...], x], axis=0)
    tail_ref[...] = x[tb - 8:, :]
    xc = cb_ref[...] + sum(cw_ref[j:j + 1, :] * ext[8 - (CONV_K - 1) + j: 8 - (CONV_K - 1) + j + tb, :]
                           for j in range(CONV_K))
    xcb = xc.astype(BF16)
    r = jnp.concatenate([jnp.dot(xcb[:, i * blk:(i + 1) * blk], wa_ref[i], preferred_element_type=F32)
                         for i in range(LRU_BLOCKS)], axis=1)
    gi = jnp.concatenate([jnp.dot(xcb[:, i * blk:(i + 1) * blk], wx_ref[i], preferred_element_type=F32)
                          for i in range(LRU_BLOCKS)], axis=1)
    r = _sigmoid(r + ba_ref[...])
    gi = _sigmoid(gi + bx_ref[...])
    log_a = -LRU_C * r * _softplus(-lam_ref[...])
    a = jnp.exp(log_a)
    a_ref[...] = a
    b_ref[...] = jnp.sqrt(-jnp.tanh(log_a) * (a * a + 1.0)) * (gi * xc)

    def body(t, h):
        h = a_ref[pl.ds(t, 1), :] * h + b_ref[pl.ds(t, 1), :]
        b_ref[pl.ds(t, 1), :] = h
        return h

    h_ref[...] = lax.fori_loop(0, tb, body, h_ref[...], unroll=8)
    o_ref[...] = (b_ref[...] * _gelu(gate_ref[...])).astype(o_ref.dtype)


def lru_branch(proj, x_col, gate_col, conv_w, conv_b, w_a, b_a, w_x, b_x, lam, tb):
    s = proj.shape[0]
    width = conv_w.shape[1]
    fixed2 = lambda i: (0, 0)
    fixed3 = lambda i: (0, 0, 0)
    vec = lambda v: v.reshape(1, width).astype(F32)
    return pl.pallas_call(
        _lru_kernel,
        grid=(s // tb,),
        in_specs=[pl.BlockSpec((tb, width), lambda i: (i, x_col)), pl.BlockSpec((tb, width), lambda i: (i, gate_col)),
                  pl.BlockSpec((CONV_K, width), fixed2), pl.BlockSpec((1, width), fixed2),
                  pl.BlockSpec(w_a.shape, fixed3), pl.BlockSpec((1, width), fixed2),
                  pl.BlockSpec(w_x.shape, fixed3), pl.BlockSpec((1, width), fixed2),
                  pl.BlockSpec((1, width), fixed2)],
        out_specs=pl.BlockSpec((tb, width), lambda i: (i, 0)),
        out_shape=jax.ShapeDtypeStruct((s, width), BF16),
        scratch_shapes=[pltpu.VMEM((8, width), F32), pltpu.VMEM((1, width), F32),
                        pltpu.VMEM((tb, width), F32), pltpu.VMEM((tb, width), F32)],
        compiler_params=_cparams("arbitrary"),
        name="rglru",
    )(proj, proj, conv_w.astype(F32), vec(conv_b), w_a.astype(BF16), vec(b_a), w_x.astype(BF16), vec(b_x), vec(lam))


def _causal_dwconv(x, w):
    return lax.conv_general_dilated(
        x, w[:, None, :].astype(x.dtype), window_strides=(1,),
        padding=[(w.shape[0] - 1, 0)], dimension_numbers=('NWC', 'WIO', 'NWC'),
        feature_group_count=x.shape[-1])


def _l2_normalize(t):
    return t * lax.rsqrt(jnp.sum(t * t, axis=-1, keepdims=True) + 1e-6)


def _chunk_gated_delta_rule(q, k, v, g, beta):
    bsz, nh, seq, dk = q.shape
    dv = v.shape[-1]
    c = GDN_CHUNK
    n = seq // c
    q = q.reshape(bsz, nh, n, c, dk)
    k = k.reshape(bsz, nh, n, c, dk)
    v = v.reshape(bsz, nh, n, c, dv)
    beta = beta.reshape(bsz, nh, n, c)
    g = jnp.cumsum(g.reshape(bsz, nh, n, c), axis=-1)
    causal = jnp.tril(jnp.ones((c, c), dtype=bool))
    strict = jnp.tril(jnp.ones((c, c), dtype=bool), k=-1)
    decay = jnp.exp(jnp.where(causal, g[..., :, None] - g[..., None, :], -jnp.inf))
    k_beta = k * beta[..., None]
    v_beta = v * beta[..., None]
    a_mat = jnp.where(strict, jnp.einsum('bhnid,bhnjd->bhnij', k_beta, k) * decay, 0.0)
    eye = jnp.eye(c, dtype=F32)
    t_inv = lax.linalg.triangular_solve(a_mat + eye, jnp.broadcast_to(eye, a_mat.shape),
                                        left_side=True, lower=True, unit_diagonal=True)
    w = jnp.einsum('bhnij,bhnjd->bhnid', t_inv, k_beta * jnp.exp(g)[..., None])
    u = jnp.einsum('bhnij,bhnjd->bhnid', t_inv, v_beta)
    intra = jnp.where(causal, jnp.einsum('bhnid,bhnjd->bhnij', q, k) * decay, 0.0)
    chunk_first = lambda t: jnp.moveaxis(t, 2, 0)

    def step(state, inp):
        q_c, k_c, u_c, w_c, g_c, a_c = inp
        v_new = u_c - jnp.einsum('bhck,bhkv->bhcv', w_c, state)
        o_c = (jnp.einsum('bhck,bhkv->bhcv', q_c * jnp.exp(g_c)[..., None], state)
               + jnp.einsum('bhcj,bhjv->bhcv', a_c, v_new))
        g_last = g_c[..., -1:]
        k_dec = k_c * jnp.exp(g_last - g_c)[..., None]
        state = state * jnp.exp(g_last)[..., None] + jnp.einsum('bhck,bhcv->bhkv', k_dec, v_new)
        return state, o_c

    state0 = jnp.zeros((bsz, nh, dk, dv), F32)
    _, o = lax.scan(step, state0, (chunk_first(q), chunk_first(k), chunk_first(u),
                                   chunk_first(w), chunk_first(g), chunk_first(intra)))
    return jnp.moveaxis(o, 0, 2).reshape(bsz, nh, seq, dv)


def gdn_branch(qkv, o_gate, beta_in, a_in, conv_w, a_log, dt_bias, norm_g):
    qkv, o_gate, beta_in, a_in = qkv[None], o_gate[None], beta_in[None], a_in[None]
    bsz, seq, _ = qkv.shape
    nqk = GDN_HEADS * GDN_DK
    qkv = jax.nn.silu(_causal_dwconv(qkv, conv_w).astype(F32))
    q, k, v = jnp.split(qkv, (nqk, 2 * nqk), axis=-1)
    heads = lambda t, d: t.reshape(bsz, seq, GDN_HEADS, d).transpose(0, 2, 1, 3)
    q = _l2_normalize(heads(q, GDN_DK)) * GDN_DK ** -0.5
    k = _l2_normalize(heads(k, GDN_DK))
    v = heads(v, GDN_DV)
    beta = jax.nn.sigmoid(beta_in.astype(F32)).transpose(0, 2, 1)
    g = (-jnp.exp(a_log.astype(F32)) * jax.nn.softplus(a_in.astype(F32) + dt_bias.astype(F32))).transpose(0, 2, 1)
    o = _chunk_gated_delta_rule(q, k, v, g, beta).transpose(0, 2, 1, 3)
    o = o * lax.rsqrt(jnp.mean(o * o, axis=-1, keepdims=True) + 1e-6) * norm_g.astype(F32)
    o = o * jax.nn.silu(o_gate.astype(F32).reshape(bsz, seq, GDN_HEADS, GDN_DV))
    return o.reshape(seq, GDN_HEADS * GDN_DV)


def _merge_kernel(ya_ref, yb_ref, yc_ref, wb_ref, ga_ref, gb_ref, gc_ref, o_ref):
    acc = None
    for n, (y_ref, g_ref) in enumerate(((ya_ref, ga_ref), (yb_ref, gb_ref), (yc_ref, gc_ref))):
        d = jnp.dot(y_ref[...], wb_ref[n], preferred_element_type=F32) * _sigmoid(g_ref[...])
        acc = d if acc is None else acc + d
    o_ref[...] = acc.astype(o_ref.dtype)


def merge_branches(ya, yb, yc, w_branch, proj, gate_col0, tm, tn):
    s, width = ya.shape
    d = w_branch.shape[2]
    nj = d // tn
    c0 = gate_col0 // tn
    yspec = pl.BlockSpec((tm, width), lambda i, j: (i, 0))
    gspec = lambda n: pl.BlockSpec((tm, tn), lambda i, j: (i, c0 + n * nj + j))
    return pl.pallas_call(
        _merge_kernel,
        grid=(s // tm, nj),
        in_specs=[yspec, yspec, yspec, pl.BlockSpec((N_BRANCH, width, tn), lambda i, j: (0, 0, j)),
                  gspec(0), gspec(1), gspec(2)],
        out_specs=pl.BlockSpec((tm, tn), lambda i, j: (i, j)),
        out_shape=jax.ShapeDtypeStruct((s, d), BF16),
        compiler_params=_cparams("parallel", "parallel"),
        name="merge_branches",
    )(ya, yb, yc, w_branch, proj, proj, proj)


def _xattn_kernel(q_ref, k_ref, v_ref, o_ref):
    d = q_ref.shape[1]
    hd = d // XA_HEADS
    outs = []
    for h in range(XA_HEADS):
        sl = slice(h * hd, (h + 1) * hd)
        s = lax.dot_general(q_ref[:, sl], k_ref[:, sl], (((1,), (1,)), ((), ())),
                            preferred_element_type=F32) * hd ** -0.5
        s = s - jnp.max(s, axis=-1, keepdims=True)
        p = jnp.exp(s)
        p = p / jnp.sum(p, axis=-1, keepdims=True)
        outs.append(jnp.dot(p.astype(BF16), v_ref[:, sl], preferred_element_type=F32))
    o_ref[...] = jnp.concatenate(outs, axis=1).astype(o_ref.dtype)


def cross_attention(q, k, v, tq):
    s, d = q.shape
    m = k.shape[0]
    return pl.pallas_call(
        _xattn_kernel,
        grid=(s // tq,),
        in_specs=[pl.BlockSpec((tq, d), lambda i: (i, 0)), pl.BlockSpec((m, d), lambda i: (0, 0)),
                  pl.BlockSpec((m, d), lambda i: (0, 0))],
        out_specs=pl.BlockSpec((tq, d), lambda i: (i, 0)),
        out_shape=jax.ShapeDtypeStruct((s, d), BF16),
        compiler_params=_cparams("parallel"),
        name="cross_attention",
    )(q, k, v)


def _peer_scores_kernel(x_ref, wq_ref, keys_ref, o_ref):
    q = jnp.dot(x_ref[...], wq_ref[...], preferred_element_type=F32).astype(BF16)
    nblk, dk, nk = keys_ref.shape
    for b in range(nblk):
        o_ref[:, b * nk:(b + 1) * nk] = jnp.dot(q[:, b * dk:(b + 1) * dk], keys_ref[b], preferred_element_type=F32)


def peer_scores(xb, wq, keys_t, tm):
    s, d = xb.shape
    nblk, dk, nk = keys_t.shape
    return pl.pallas_call(
        _peer_scores_kernel,
        grid=(s // tm,),
        in_specs=[pl.BlockSpec((tm, d), lambda i: (i, 0)), pl.BlockSpec(wq.shape, lambda i: (0, 0)),
                  pl.BlockSpec(keys_t.shape, lambda i: (0, 0, 0))],
        out_specs=pl.BlockSpec((tm, nblk * nk), lambda i: (i, 0)),
        out_shape=jax.ShapeDtypeStruct((s, nblk * nk), F32),
        compiler_params=_cparams("parallel"),
        name="peer_scores",
    )(xb, wq, keys_t)


def peer_route(scores):
    s = scores.shape[0]
    sc = scores.reshape(s, PEER_HEADS, 2, PEER_NKEYS)
    top_s, top_i = lax.top_k(sc, PEER_TOPK)
    cand = top_s[:, :, 0, :, None] + top_s[:, :, 1, None, :]
    best_s, _ = lax.top_k(cand.reshape(s, PEER_HEADS, PEER_TOPK * PEER_TOPK), PEER_TOPK)
    tau = best_s[..., PEER_TOPK - 1]
    norm = jnp.sum(jnp.exp(best_s - best_s[..., :1]), axis=-1)
    s1, s2 = sc[:, :, 0], sc[:, :, 1]
    in1 = s1 >= top_s[:, :, 0, PEER_TOPK - 1:]
    in2 = s2 >= top_s[:, :, 1, PEER_TOPK - 1:]
    e1 = jnp.where(in1, jnp.exp(s1 - top_s[:, :, 0, :1]), 0.0)
    e2 = jnp.where(in2, jnp.exp(s2 - top_s[:, :, 1, :1]) / norm[..., None], 0.0)
    theta_rank = jnp.min(jnp.where(cand >= tau[..., None, None], top_s[:, :, 1, None, :], POS_BIG), axis=-1)
    onehot = top_i[:, :, 0, :, None] == jnp.arange(PEER_NKEYS)[None, None, None, :]
    theta = jnp.min(jnp.where(onehot, theta_rank[..., None], POS_BIG), axis=2)
    s2m = jnp.where(in2, s2, NEG_BIG)
    flat = lambda t: t.reshape(s, PEER_HEADS * PEER_NKEYS)
    return flat(e1), flat(theta), flat(s2m), flat(e2)


def _peer_dense_kernel(x_ref, u_ref, vt_ref, e1_ref, th_ref, s2_ref, e2_ref, o_ref):
    @pl.when(pl.program_id(1) == 0)
    def _():
        o_ref[...] = jnp.zeros_like(o_ref)

    te = u_ref.shape[0]
    na = te // PEER_NKEYS
    a0 = pl.program_id(1) * na
    ht = lax.dot_general(u_ref[...], x_ref[...], (((1,), (1,)), ((), ())), preferred_element_type=F32)
    acts = []
    for ai in range(na):
        w = None
        for hd in range(PEER_HEADS):
            rows = slice(hd * PEER_NKEYS, (hd + 1) * PEER_NKEYS)
            row = hd * PEER_NKEYS + a0 + ai
            th = th_ref[pl.ds(row, 1), :]
            e1 = e1_ref[pl.ds(row, 1), :]
            term = jnp.where(s2_ref[rows, :] >= th, e2_ref[rows, :], 0.0) * e1
            w = term if w is None else w + term
        acts.append((_gelu(ht[ai * PEER_NKEYS:(ai + 1) * PEER_NKEYS, :]) * w).astype(BF16))
    act = jnp.concatenate(acts, axis=0)
    o_ref[...] += jnp.dot(vt_ref[...], act, preferred_element_type=F32)


def peer_dense(xb, u_tab, vt_tab, e1t, thetat, s2mt, e2t, tt, te):
    s, d = xb.shape
    ne = u_tab.shape[0]
    hk = e1t.shape[0]
    side = pl.BlockSpec((hk, tt), lambda i, j: (0, i))
    return pl.pallas_call(
        _peer_dense_kernel,
        grid=(s // tt, ne // te),
        in_specs=[pl.BlockSpec((tt, d), lambda i, j: (i, 0)), pl.BlockSpec((te, d), lambda i, j: (j, 0)),
                  pl.BlockSpec((d, te), lambda i, j: (0, j)), side, side, side, side],
        out_specs=pl.BlockSpec((d, tt), lambda i, j: (0, i)),
        out_shape=jax.ShapeDtypeStruct((d, s), F32),
        compiler_params=_cparams("parallel", "arbitrary"),
        name="peer_dense",
    )(xb, u_tab, vt_tab, e1t, thetat, s2mt, e2t)


def _res_ln_t_kernel(ht_ref, x_ref, g_ref, b_ref, o_ref, ob_ref):
    y = _layer_norm(DN_ALPHA * x_ref[...] + ht_ref[...].T, g_ref[...], b_ref[...])
    o_ref[...] = y
    ob_ref[...] = y.astype(BF16)


def res_ln_t(ht, x, g, b, tm, name):
    m, n = x.shape
    row = lambda i: (i, 0)
    fixed = lambda i: (0, 0)
    return pl.pallas_call(
        _res_ln_t_kernel,
        grid=(m // tm,),
        in_specs=[pl.BlockSpec((n, tm), lambda i: (0, i)), pl.BlockSpec((tm, n), row),
                  pl.BlockSpec((1, n), fixed), pl.BlockSpec((1, n), fixed)],
        out_specs=[pl.BlockSpec((tm, n), row), pl.BlockSpec((tm, n), row)],
        out_shape=[jax.ShapeDtypeStruct((m, n), F32), jax.ShapeDtypeStruct((m, n), BF16)],
        compiler_params=_cparams("parallel"),
        name=name,
    )(ht, x, g.reshape(1, n), b.reshape(1, n))


def _layer(x, xb, memb, p):
    s, d = x.shape
    half = d // 2
    w_in = p['w_in']
    n_main = 7 * half
    n_small = 2 * GDN_HEADS
    w_cat = jnp.concatenate([w_in[:, :n_main], w_in[:, n_main + n_small:], w_in[:, n_main:n_main + n_small],
                             jnp.zeros((d, 128 - n_small), w_in.dtype)], axis=1).astype(BF16)
    proj = mm(xb, w_cat, F32, 1024, 640, "in_proj")
    gate_col0 = n_main
    small0 = n_main + N_BRANCH * d

    g_s5 = half // S5_GROUP
    nchunk = s // S5_CHUNK
    u_r = (proj[:, :half].astype(BF16).reshape(nchunk, S5_CHUNK, g_s5, S5_GROUP)
           .transpose(2, 0, 1, 3).reshape(g_s5, nchunk, S5_CHUNK * S5_GROUP))
    prep = s5_prepare(p['s5_a_re'], p['s5_a_im'], p['s5_log_dt'], p['s5_b_re'], p['s5_b_im'],
                      p['s5_c_re'], p['s5_c_im'], p['s5_d'])
    y_r = s5_branch(u_r, *prep)
    y_s5 = (y_r.reshape(g_s5, nchunk, S5_CHUNK, S5_GROUP).transpose(1, 2, 0, 3).reshape(s, half))
    y_a = glu(y_s5, p['s5_w_glu'].astype(BF16), p['s5_b_glu'].astype(F32), 512)

    y_b = lru_branch(proj, 1, 2, p['lru_conv_w'], p['lru_conv_b'], p['lru_w_a'], p['lru_b_a'],
                     p['lru_w_x'], p['lru_b_x'], p['lru_lambda'], 256)

    y_c = gdn_branch(proj[:, 3 * half:6 * half], proj[:, 6 * half:7 * half],
                     proj[:, small0:small0 + GDN_HEADS], proj[:, small0 + GDN_HEADS:small0 + 2 * GDN_HEADS],
                     p['gdn_conv_w'], p['gdn_a_log'], p['gdn_dt_bias'], p['gdn_norm_g']).astype(BF16)

    merged = merge_branches(y_a, y_b, y_c, p['w_branch'].astype(BF16), proj, gate_col0, 1024, 512)
    x, xb = mm_res_ln(merged, p['w_out'].astype(BF16), x, p['ln_mix_g'], p['ln_mix_b'], 512, "out_proj_ln")

    q = mm(xb, p['xa_wq'].astype(BF16), BF16, 1024, 512, "xa_q")
    k = mm(memb, p['xa_wk'].astype(BF16), BF16, memb.shape[0], 512, "xa_k")
    v = mm(memb, p['xa_wv'].astype(BF16), BF16, memb.shape[0], 512, "xa_v")
    o = cross_attention(q, k, v, 512)
    x, xb = mm_res_ln(o, p['xa_wo'].astype(BF16), x, p['ln_xa_g'], p['ln_xa_b'], 512, "xa_out_ln")

    keys = p['peer_keys']
    keys_t = keys.reshape(PEER_HEADS * 2, PEER_NKEYS, keys.shape[-1]).transpose(0, 2, 1).astype(BF16)
    scores = peer_scores(xb, p['peer_wq'].astype(BF16), keys_t, 512)
    e1, theta, s2m, e2 = peer_route(scores)
    ht = peer_dense(xb, p['peer_u'].astype(BF16), p['peer_v'].T.astype(BF16), e1.T, theta.T, s2m.T, e2.T, 512, 512)
    return res_ln_t(ht, x, p['ln_ffn_g'], p['ln_ffn_b'], 512, "ffn_ln")


def kernel(x, mem, ln_mix_g, ln_mix_b, w_in, s5_a_re, s5_a_im, s5_log_dt, s5_b_re, s5_b_im, s5_c_re, s5_c_im, s5_d, s5_w_glu, s5_b_glu, lru_conv_w, lru_conv_b, lru_w_a, lru_b_a, lru_w_x, lru_b_x, lru_lambda, gdn_conv_w, gdn_a_log, gdn_dt_bias, gdn_norm_g, w_branch, w_out, ln_xa_g, ln_xa_b, xa_wq, xa_wk, xa_wv, xa_wo, ln_ffn_g, ln_ffn_b, peer_wq, peer_keys, peer_u, peer_v):
    params = dict(ln_mix_g=ln_mix_g, ln_mix_b=ln_mix_b, w_in=w_in, s5_a_re=s5_a_re, s5_a_im=s5_a_im,
                  s5_log_dt=s5_log_dt, s5_b_re=s5_b_re, s5_b_im=s5_b_im, s5_c_re=s5_c_re, s5_c_im=s5_c_im,
                  s5_d=s5_d, s5_w_glu=s5_w_glu, s5_b_glu=s5_b_glu, lru_conv_w=lru_conv_w, lru_conv_b=lru_conv_b,
                  lru_w_a=lru_w_a, lru_b_a=lru_b_a, lru_w_x=lru_w_x, lru_b_x=lru_b_x, lru_lambda=lru_lambda,
                  gdn_conv_w=gdn_conv_w, gdn_a_log=gdn_a_log, gdn_dt_bias=gdn_dt_bias, gdn_norm_g=gdn_norm_g,
                  w_branch=w_branch, w_out=w_out, ln_xa_g=ln_xa_g, ln_xa_b=ln_xa_b, xa_wq=xa_wq, xa_wk=xa_wk,
                  xa_wv=xa_wv, xa_wo=xa_wo, ln_ffn_g=ln_ffn_g, ln_ffn_b=ln_ffn_b, peer_wq=peer_wq,
                  peer_keys=peer_keys, peer_u=peer_u, peer_v=peer_v)
    bsz, seq, d = x.shape
    outs = []
    for b in range(bsz):
        xf = x[b].astype(F32)
        xb = xf.astype(BF16)
        memb = mem[b].astype(BF16)
        for l in range(DEPTH):
            xf, xb = _layer(xf, xb, memb, {k: v[l] for k, v in params.items()})
        outs.append(xf)
    return jnp.stack(outs, axis=0)
```

```python
import functools
import math

import jax
import jax.numpy as jnp
from jax import lax
from jax.experimental import pallas as pl
from jax.experimental.pallas import tpu as pltpu

F32 = jnp.float32
BF16 = jnp.bfloat16

S5_GROUP = 16
S5_STATE = 64
S5_CHUNK = 16
LRU_BLOCKS = 8
LRU_C = 8.0
CONV_K = 4
GDN_HEADS = 8
GDN_DK = 128
GDN_DV = 128
GDN_CHUNK = 64
N_BRANCH = 3
XA_HEADS = 4
PEER_HEADS = 8
PEER_NKEYS = 128
PEER_TOPK = 16
DEPTH = 2
DN_ALPHA = (2 * DEPTH) ** 0.25
LN_EPS = 1e-5
NEG_BIG = -1e30
POS_BIG = 1e30
VMEM_LIMIT = 56 * 1024 * 1024


def _cparams(*sem):
    return pltpu.CompilerParams(dimension_semantics=sem, vmem_limit_bytes=VMEM_LIMIT)


def _gelu(x):
    return 0.5 * x * (1.0 + jnp.tanh(math.sqrt(2.0 / math.pi) * (x + 0.044715 * (x * x * x))))


def _sigmoid(x):
    return 1.0 / (1.0 + jnp.exp(-x))


def _softplus(x):
    return jnp.maximum(x, 0.0) + jnp.log1p(jnp.exp(-jnp.abs(x)))


def _layer_norm(y, g, b):
    mu = jnp.mean(y, axis=-1, keepdims=True)
    d = y - mu
    var = jnp.mean(d * d, axis=-1, keepdims=True)
    return d * lax.rsqrt(var + LN_EPS) * g + b


def _mm_kernel(a_ref, b_ref, o_ref):
    o_ref[...] = jnp.dot(a_ref[...], b_ref[...], preferred_element_type=F32).astype(o_ref.dtype)


def mm(a, b, out_dtype, tm, tn, name):
    m, k = a.shape
    n = b.shape[1]
    return pl.pallas_call(
        _mm_kernel,
        grid=(m // tm, n // tn),
        in_specs=[pl.BlockSpec((tm, k), lambda i, j: (i, 0)),
                  pl.BlockSpec((k, tn), lambda i, j: (0, j))],
        out_specs=pl.BlockSpec((tm, tn), lambda i, j: (i, j)),
        out_shape=jax.ShapeDtypeStruct((m, n), out_dtype),
        compiler_params=_cparams("parallel", "parallel"),
        name=name,
    )(a, b)


def _mm_res_ln_kernel(a_ref, w_ref, x_ref, g_ref, b_ref, o_ref, ob_ref):
    h = jnp.dot(a_ref[...], w_ref[...], preferred_element_type=F32)
    y = _layer_norm(DN_ALPHA * x_ref[...] + h, g_ref[...], b_ref[...])
    o_ref[...] = y
    ob_ref[...] = y.astype(BF16)


def mm_res_ln(a, w, x, g, b, tm, name):
    m, k = a.shape
    n = w.shape[1]
    row = lambda i: (i, 0)
    fixed = lambda i: (0, 0)
    return pl.pallas_call(
        _mm_res_ln_kernel,
        grid=(m // tm,),
        in_specs=[pl.BlockSpec((tm, k), row), pl.BlockSpec((k, n), fixed), pl.BlockSpec((tm, n), row),
                  pl.BlockSpec((1, n), fixed), pl.BlockSpec((1, n), fixed)],
        out_specs=[pl.BlockSpec((tm, n), row), pl.BlockSpec((tm, n), row)],
        out_shape=[jax.ShapeDtypeStruct((m, n), F32), jax.ShapeDtypeStruct((m, n), BF16)],
        compiler_params=_cparams("parallel"),
        name=name,
    )(a, w, x, g.reshape(1, n), b.reshape(1, n))


def _res_ln_kernel(h_ref, x_ref, g_ref, b_ref, o_ref, ob_ref):
    y = _layer_norm(DN_ALPHA * x_ref[...] + h_ref[...], g_ref[...], b_ref[...])
    o_ref[...] = y
    ob_ref[...] = y.astype(BF16)


def res_ln(h, x, g, b, tm, name):
    m, n = x.shape
    row = lambda i: (i, 0)
    fixed = lambda i: (0, 0)
    return pl.pallas_call(
        _res_ln_kernel,
        grid=(m // tm,),
        in_specs=[pl.BlockSpec((tm, n), row), pl.BlockSpec((tm, n), row),
                  pl.BlockSpec((1, n), fixed), pl.BlockSpec((1, n), fixed)],
        out_specs=[pl.BlockSpec((tm, n), row), pl.BlockSpec((tm, n), row)],
        out_shape=[jax.ShapeDtypeStruct((m, n), F32), jax.ShapeDtypeStruct((m, n), BF16)],
        compiler_params=_cparams("parallel"),
        name=name,
    )(h, x, g.reshape(1, n), b.reshape(1, n))


def s5_prepare(a_re, a_im, log_dt, b_re, b_im, c_re, c_im, d_skip):
    hi = lax.Precision.HIGHEST
    L = S5_CHUNK
    lam_re, lam_im = a_re.astype(F32), a_im.astype(F32)
    dt = jnp.exp(log_dt.astype(F32))[:, None]
    mag = jnp.exp(lam_re * dt)
    abar_re, abar_im = mag * jnp.cos(lam_im * dt), mag * jnp.sin(lam_im * dt)
    den = lam_re * lam_re + lam_im * lam_im
    num_re = abar_re - 1.0
    f_re = (num_re * lam_re + abar_im * lam_im) / den
    f_im = (abar_im * lam_re - num_re * lam_im) / den
    bre, bim = b_re.astype(F32), b_im.astype(F32)
    bbar_re = f_re[..., None] * bre - f_im[..., None] * bim
    bbar_im = f_re[..., None] * bim + f_im[..., None] * bre
    steps = jnp.arange(L + 1, dtype=F32)[:, None, None]
    pw_mag = jnp.exp(lam_re * dt * steps)
    pw_re = pw_mag * jnp.cos(lam_im * dt * steps)
    pw_im = pw_mag * jnp.sin(lam_im * dt * steps)
    cre, cim = c_re.astype(F32), c_im.astype(F32)
    ca_re = cre[None] * pw_re[:, :, None, :] - cim[None] * pw_im[:, :, None, :]
    ca_im = cre[None] * pw_im[:, :, None, :] + cim[None] * pw_re[:, :, None, :]
    kern = (jnp.einsum('dgpn,gnq->dgpq', ca_re[:L], bbar_re, precision=hi)
            - jnp.einsum('dgpn,gnq->dgpq', ca_im[:L], bbar_im, precision=hi))
    kern = kern.at[0].add(d_skip.astype(F32)[:, :, None] * jnp.eye(S5_GROUP, dtype=F32)[None])
    lag = jnp.arange(L)[None, :] - jnp.arange(L)[:, None]
    t5 = jnp.where((lag >= 0)[:, :, None, None, None], kern[jnp.clip(lag, 0, L - 1)], 0.0)
    g = a_re.shape[0]
    t_mat = t5.transpose(2, 0, 4, 1, 3).reshape(g, L * S5_GROUP, L * S5_GROUP)
    rev_re, rev_im = pw_re[L - 1::-1][:L], pw_im[L - 1::-1][:L]
    bc_re = rev_re[..., None] * bbar_re[None] - rev_im[..., None] * bbar_im[None]
    bc_im = rev_re[..., None] * bbar_im[None] + rev_im[..., None] * bbar_re[None]
    bc_mat = jnp.concatenate([bc_re.transpose(1, 0, 3, 2), bc_im.transpose(1, 0, 3, 2)],
                             axis=-1).reshape(g, L * S5_GROUP, 2 * S5_STATE)
    cc_re = ca_re[1:].transpose(1, 3, 0, 2)
    cc_im = -ca_im[1:].transpose(1, 3, 0, 2)
    cc_mat = jnp.concatenate([cc_re, cc_im], axis=1).reshape(g, 2 * S5_STATE, L * S5_GROUP)
    a1 = jnp.concatenate([pw_re[L], pw_re[L]], axis=-1)
    a2 = jnp.concatenate([-pw_im[L], pw_im[L]], axis=-1)
    return t_mat.astype(BF16), bc_mat.astype(BF16), cc_mat.astype(BF16), a1, a2


def _s5_z_kernel(u_ref, bc_ref, z_ref):
    z_ref[...] = jnp.dot(u_ref[0], bc_ref[0], preferred_element_type=F32)


def _s5_scan_kernel(z_ref, a1_ref, a2_ref, s_ref, carry_ref):
    @pl.when(pl.program_id(0) == 0)
    def _():
        carry_ref[...] = jnp.zeros_like(carry_ref)

    a1 = a1_ref[...]
    a2 = a2_ref[...]

    def body(c, state):
        s_ref[c] = state
        swapped = pltpu.roll(state, S5_STATE, axis=1)
        return a1 * state + a2 * swapped + z_ref[c]

    carry_ref[...] = lax.fori_loop(0, z_ref.shape[0], body, carry_ref[...])


def _s5_out_kernel(u_ref, t_ref, s_ref, cc_ref, y_ref):
    y = jnp.dot(u_ref[0], t_ref[0], preferred_element_type=F32)
    y = y + jnp.dot(s_ref[...].astype(BF16), cc_ref[0], preferred_element_type=F32)
    y_ref[0] = _gelu(y)


def s5_branch(u_r, t_mat, bc_mat, cc_mat, a1, a2):
    g, c, w = u_r.shape
    ns = 2 * S5_STATE
    z = pl.pallas_call(
        _s5_z_kernel,
        grid=(g,),
        in_specs=[pl.BlockSpec((1, c, w), lambda i: (i, 0, 0)), pl.BlockSpec((1, w, ns), lambda i: (i, 0, 0))],
        out_specs=pl.BlockSpec((c, ns), lambda i: (0, i)),
        out_shape=jax.ShapeDtypeStruct((c, g * ns), F32),
        compiler_params=_cparams("parallel"),
        name="s5_chunk_inputs",
    )(u_r, bc_mat)
    cb = min(64, c)
    s_prev = pl.pallas_call(
        _s5_scan_kernel,
        grid=(c // cb,),
        in_specs=[pl.BlockSpec((cb, g, ns), lambda i: (i, 0, 0)),
                  pl.BlockSpec((g, ns), lambda i: (0, 0)), pl.BlockSpec((g, ns), lambda i: (0, 0))],
        out_specs=pl.BlockSpec((cb, g, ns), lambda i: (i, 0, 0)),
        out_shape=jax.ShapeDtypeStruct((c, g, ns), F32),
        scratch_shapes=[pltpu.VMEM((g, ns), F32)],
        compiler_params=_cparams("arbitrary"),
        name="s5_chunk_scan",
    )(z.reshape(c, g, ns), a1, a2)
    return pl.pallas_call(
        _s5_out_kernel,
        grid=(g,),
        in_specs=[pl.BlockSpec((1, c, w), lambda i: (i, 0, 0)), pl.BlockSpec((1, w, w), lambda i: (i, 0, 0)),
                  pl.BlockSpec((c, ns), lambda i: (0, i)), pl.BlockSpec((1, ns, w), lambda i: (i, 0, 0))],
        out_specs=pl.BlockSpec((1, c, w), lambda i: (i, 0, 0)),
        out_shape=jax.ShapeDtypeStruct((g, c, w), F32),
        compiler_params=_cparams("parallel"),
        name="s5_outputs",
    )(u_r, t_mat, s_prev.reshape(c, g * ns), cc_mat)


def _glu_kernel(y_ref, w_ref, b_ref, o_ref):
    y = y_ref[...]
    gate = jnp.dot(y.astype(BF16), w_ref[...], preferred_element_type=F32) + b_ref[...]
    o_ref[...] = (y * _sigmoid(gate)).astype(o_ref.dtype)


def glu(y, w, b, tm):
    m, n = y.shape
    return pl.pallas_call(
        _glu_kernel,
        grid=(m // tm,),
        in_specs=[pl.BlockSpec((tm, n), lambda i: (i, 0)), pl.BlockSpec((n, n), lambda i: (0, 0)),
                  pl.BlockSpec((1, n), lambda i: (0, 0))],
        out_specs=pl.BlockSpec((tm, n), lambda i: (i, 0)),
        out_shape=jax.ShapeDtypeStruct((m, n), BF16),
        compiler_params=_cparams("parallel"),
        name="s5_glu",
    )(y, w, b.reshape(1, n))


def _lru_kernel(x_ref, gate_ref, cw_ref, cb_ref, wa_ref, ba_ref, wx_ref, bx_ref, lam_ref, o_ref,
                tail_ref, h_ref, a_ref, b_ref):
    tb, width = x_ref.shape
    blk = width // LRU_BLOCKS

    @pl.when(pl.program_id(0) == 0)
    def _():
        tail_ref[...] = jnp.zeros_like(tail_ref)
        h_ref[...] = jnp.zeros_like(h_ref)

    x = x_ref[...]
    ext = jnp.concatenate([tail_ref[...], x], axis=0)
    tail_ref[...] = x[tb - 8:, :]
    xc = cb_ref[...] + sum(cw_ref[j:j + 1, :] * ext[8 - (CONV_K - 1) + j: 8 - (CONV_K - 1) + j + tb, :]
                           for j in range(CONV_K))
    xcb = xc.astype(BF16)
    r = jnp.concatenate([jnp.dot(xcb[:, i * blk:(i + 1) * blk], wa_ref[i], preferred_element_type=F32)
                         for i in range(LRU_BLOCKS)], axis=1)
    gi = jnp.concatenate([jnp.dot(xcb[:, i * blk:(i + 1) * blk], wx_ref[i], preferred_element_type=F32)
                          for i in range(LRU_BLOCKS)], axis=1)
    r = _sigmoid(r + ba_ref[...])
    gi = _sigmoid(gi + bx_ref[...])
    log_a = -LRU_C * r * _softplus(-lam_ref[...])
    a = jnp.exp(log_a)
    a_ref[...] = a
    b_ref[...] = jnp.sqrt(-jnp.tanh(log_a) * (a * a + 1.0)) * (gi * xc)

    def body(t, h):
        h = a_ref[pl.ds(t, 1), :] * h + b_ref[pl.ds(t, 1), :]
        b_ref[pl.ds(t, 1), :] = h
        return h

    h_ref[...] = lax.fori_loop(0, tb, body, h_ref[...], unroll=8)
    o_ref[...] = (b_ref[...] * _gelu(gate_ref[...])).astype(o_ref.dtype)


def lru_branch(proj, x_col, gate_col, conv_w, conv_b, w_a, b_a, w_x, b_x, lam, tb):
    s = proj.shape[0]
    width = conv_w.shape[1]
    fixed2 = lambda i: (0, 0)
    fixed3 = lambda i: (0, 0, 0)
    vec = lambda v: v.reshape(1, width).astype(F32)
    return pl.pallas_call(
        _lru_kernel,
        grid=(s // tb,),
        in_specs=[pl.BlockSpec((tb, width), lambda i: (i, x_col)), pl.BlockSpec((tb, width), lambda i: (i, gate_col)),
                  pl.BlockSpec((CONV_K, width), fixed2), pl.BlockSpec((1, width), fixed2),
                  pl.BlockSpec(w_a.shape, fixed3), pl.BlockSpec((1, width), fixed2),
                  pl.BlockSpec(w_x.shape, fixed3), pl.BlockSpec((1, width), fixed2),
                  pl.BlockSpec((1, width), fixed2)],
        out_specs=pl.BlockSpec((tb, width), lambda i: (i, 0)),
        out_shape=jax.ShapeDtypeStruct((s, width), BF16),
        scratch_shapes=[pltpu.VMEM((8, width), F32), pltpu.VMEM((1, width), F32),
                        pltpu.VMEM((tb, width), F32), pltpu.VMEM((tb, width), F32)],
        compiler_params=_cparams("arbitrary"),
        name="rglru",
    )(proj, proj, conv_w.astype(F32), vec(conv_b), w_a.astype(BF16), vec(b_a), w_x.astype(BF16), vec(b_x), vec(lam))


def _split3(x):
    hi = x.astype(BF16)
    r1 = x - hi.astype(F32)
    mid = r1.astype(BF16)
    lo = (r1 - mid.astype(F32)).astype(BF16)
    return hi, mid, lo


def _nt(a, b):
    return lax.dot_general(a, b, (((1,), (1,)), ((), ())), preferred_element_type=F32)


def _gdn_kernel(qkv_ref, og_ref, ba_ref, cw_ref, alog_ref, dtb_ref, ng_ref, o_ref,
                tail_ref, state_ref, q_s, k_s, v_s, beta_s, g_s):
    tb = qkv_ref.shape[0]
    c = GDN_CHUNK
    dk, dv = GDN_DK, GDN_DV
    nqk = GDN_HEADS * dk

    @pl.when(pl.program_id(0) == 0)
    def _():
        tail_ref[...] = jnp.zeros_like(tail_ref)
        state_ref[...] = jnp.zeros_like(state_ref)

    x = qkv_ref[...]
    ext = jnp.concatenate([tail_ref[...], x], axis=0)
    tail_ref[...] = x[tb - 8:, :]
    xc = sum(cw_ref[j:j + 1, :] * ext[8 - (CONV_K - 1) + j: 8 - (CONV_K - 1) + j + tb, :] for j in range(CONV_K))
    xc = xc * _sigmoid(xc)
    for h in range(GDN_HEADS):
        qh = xc[:, h * dk:(h + 1) * dk]
        kh = xc[:, nqk + h * dk: nqk + (h + 1) * dk]
        q_s[:, h * dk:(h + 1) * dk] = qh * lax.rsqrt(jnp.sum(qh * qh, axis=-1, keepdims=True) + 1e-6) * dk ** -0.5
        k_s[:, h * dk:(h + 1) * dk] = kh * lax.rsqrt(jnp.sum(kh * kh, axis=-1, keepdims=True) + 1e-6)
    v_s[...] = xc[:, 2 * nqk:]
    ba = ba_ref[...]
    beta_s[...] = _sigmoid(ba)
    g_s[...] = -jnp.exp(alog_ref[...]) * _softplus(ba + dtb_ref[...])

    ri = lax.broadcasted_iota(jnp.int32, (c, c), 0)
    ci = lax.broadcasted_iota(jnp.int32, (c, c), 1)
    causal = ri >= ci
    strict = ri > ci
    tril = jnp.where(causal, 1.0, 0.0).astype(BF16)
    eye = jnp.where(ri == ci, 1.0, 0.0)
    lane = lax.broadcasted_iota(jnp.int32, (c, 128), 1)
    pad_x = jnp.where((lane >= 3) & (lane < 6), 1.0, 0.0)
    pad_y = jnp.where(lane < 3, 1.0, 0.0)

    def chunk(ic, _):
        r0 = pl.multiple_of(ic * c, c)
        rows = pl.ds(r0, c)
        g_parts = _split3(g_s[rows, :])
        gcum = sum(jnp.dot(tril, part, preferred_element_type=F32) for part in g_parts)
        beta = beta_s[rows, :]
        heads = range(GDN_HEADS)
        hs = [slice(h * dk, (h + 1) * dk) for h in heads]
        dot = functools.partial(jnp.dot, preferred_element_type=F32)
        gcol = [gcum[:, GDN_HEADS + h:GDN_HEADS + h + 1] for h in heads]
        q = [q_s[rows, hs[h]] for h in heads]
        k = [k_s[rows, hs[h]] for h in heads]
        kbf = [k[h].astype(BF16) for h in heads]
        kb = [k[h] * beta[:, h:h + 1] for h in heads]
        vb = [(v_s[rows, hs[h]] * beta[:, h:h + 1]).astype(BF16) for h in heads]
        decay = []
        for h in heads:
            hi, mid, lo = (p.astype(F32) for p in _split3(gcol[h]))
            xa = jnp.where(lane == 0, hi, jnp.where(lane == 1, mid, jnp.where(lane == 2, lo, pad_x)))
            ya = jnp.where(lane == 3, -hi, jnp.where(lane == 4, -mid, jnp.where(lane == 5, -lo, pad_y)))
            decay.append(jnp.exp(jnp.where(causal, _nt(xa.astype(BF16), ya.astype(BF16)), -jnp.inf)))
        neg_a = [jnp.where(strict, -_nt(kb[h].astype(BF16), kbf[h]) * decay[h], 0.0) for h in heads]
        intra = [jnp.where(causal, _nt(q[h].astype(BF16), kbf[h]) * decay[h], 0.0).astype(BF16) for h in heads]
        t_off = list(neg_a)
        pw = list(neg_a)
        for _ in range(5):
            pwb = [pw[h].astype(BF16) for h in heads]
            pw = [dot(pwb[h], pwb[h]) for h in heads]
            t_off = [t_off[h] + pw[h] + dot(t_off[h].astype(BF16), pw[h].astype(BF16)) for h in heads]
        t_inv = [(eye + t_off[h]).astype(BF16) for h in heads]
        eg = [jnp.exp(gcol[h]) for h in heads]
        w = [dot(t_inv[h], (kb[h] * eg[h]).astype(BF16)).astype(BF16) for h in heads]
        u = [dot(t_inv[h], vb[h]) for h in heads]
        state = [state_ref[h] for h in heads]
        sb = [state[h].astype(BF16) for h in heads]
        vnb = [(u[h] - dot(w[h], sb[h])).astype(BF16) for h in heads]
        g_last = [gcol[h][c - 1:c, :] for h in heads]
        k_dec = [(k[h] * jnp.exp(g_last[h] - gcol[h])).astype(BF16) for h in heads]
        for h in heads:
            state_ref[h] = state[h] * jnp.exp(g_last[h]) + lax.dot_general(
                k_dec[h], vnb[h], (((0,), (0,)), ((), ())), preferred_element_type=F32)
        o = [dot((q[h] * eg[h]).astype(BF16), sb[h]) + dot(intra[h], vnb[h]) for h in heads]
        for h in heads:
            on = o[h] * lax.rsqrt(jnp.mean(o[h] * o[h], axis=-1, keepdims=True) + 1e-6) * ng_ref[...]
            og = og_ref[rows, hs[h]]
            o_ref[rows, hs[h]] = (on * (og * _sigmoid(og))).astype(o_ref.dtype)
        return 0

    lax.fori_loop(0, tb // c, chunk, 0)


def gdn_branch_pallas(proj, qkv_col, og_col, ba_col, conv_w, a_log, dt_bias, norm_g, tb):
    s = proj.shape[0]
    nv = GDN_HEADS * GDN_DV
    nqkv = conv_w.shape[1]
    pad = jnp.zeros((GDN_HEADS,), F32)
    lanes = lambda v: jnp.concatenate([pad, v.astype(F32), jnp.zeros((128 - 2 * GDN_HEADS,), F32)]).reshape(1, 128)
    fixed = lambda i: (0, 0)
    return pl.pallas_call(
        _gdn_kernel,
        grid=(s // tb,),
        in_specs=[pl.BlockSpec((tb, nqkv), lambda i: (i, qkv_col)), pl.BlockSpec((tb, nv), lambda i: (i, og_col)),
                  pl.BlockSpec((tb, 128), lambda i: (i, ba_col)), pl.BlockSpec((CONV_K, nqkv), fixed),
                  pl.BlockSpec((1, 128), fixed), pl.BlockSpec((1, 128), fixed), pl.BlockSpec((1, GDN_DV), fixed)],
        out_specs=pl.BlockSpec((tb, nv), lambda i: (i, 0)),
        out_shape=jax.ShapeDtypeStruct((s, nv), BF16),
        scratch_shapes=[pltpu.VMEM((8, nqkv), F32), pltpu.VMEM((GDN_HEADS, GDN_DK, GDN_DV), F32),
                        pltpu.VMEM((tb, nv), F32), pltpu.VMEM((tb, nv), F32), pltpu.VMEM((tb, nv), F32),
                        pltpu.VMEM((tb, 128), F32), pltpu.VMEM((tb, 128), F32)],
        compiler_params=_cparams("arbitrary"),
        name="gated_deltanet",
    )(proj, proj, proj, conv_w.astype(F32), lanes(a_log), lanes(dt_bias), norm_g.astype(F32).reshape(1, GDN_DV))


def _merge_kernel(ya_ref, yb_ref, yc_ref, wb_ref, ga_ref, gb_ref, gc_ref, o_ref):
    acc = None
    for n, (y_ref, g_ref) in enumerate(((ya_ref, ga_ref), (yb_ref, gb_ref), (yc_ref, gc_ref))):
        d = jnp.dot(y_ref[...], wb_ref[n], preferred_element_type=F32) * _sigmoid(g_ref[...])
        acc = d if acc is None else acc + d
    o_ref[...] = acc.astype(o_ref.dtype)


def merge_branches(ya, yb, yc, w_branch, proj, gate_col0, tm, tn):
    s, width = ya.shape
    d = w_branch.shape[2]
    nj = d // tn
    c0 = gate_col0 // tn
    yspec = pl.BlockSpec((tm, width), lambda i, j: (i, 0))
    gspec = lambda n: pl.BlockSpec((tm, tn), lambda i, j: (i, c0 + n * nj + j))
    return pl.pallas_call(
        _merge_kernel,
        grid=(s // tm, nj),
        in_specs=[yspec, yspec, yspec, pl.BlockSpec((N_BRANCH, width, tn), lambda i, j: (0, 0, j)),
                  gspec(0), gspec(1), gspec(2)],
        out_specs=pl.BlockSpec((tm, tn), lambda i, j: (i, j)),
        out_shape=jax.ShapeDtypeStruct((s, d), BF16),
        compiler_params=_cparams("parallel", "parallel"),
        name="merge_branches",
    )(ya, yb, yc, w_branch, proj, proj, proj)


def _xattn_kernel(q_ref, k_ref, v_ref, o_ref):
    d = q_ref.shape[1]
    hd = d // XA_HEADS
    outs = []
    for h in range(XA_HEADS):
        sl = slice(h * hd, (h + 1) * hd)
        s = lax.dot_general(q_ref[:, sl], k_ref[:, sl], (((1,), (1,)), ((), ())),
                            preferred_element_type=F32) * hd ** -0.5
        s = s - jnp.max(s, axis=-1, keepdims=True)
        p = jnp.exp(s)
        p = p / jnp.sum(p, axis=-1, keepdims=True)
        outs.append(jnp.dot(p.astype(BF16), v_ref[:, sl], preferred_element_type=F32))
    o_ref[...] = jnp.concatenate(outs, axis=1).astype(o_ref.dtype)


def cross_attention(q, k, v, tq):
    s, d = q.shape
    m = k.shape[0]
    return pl.pallas_call(
        _xattn_kernel,
        grid=(s // tq,),
        in_specs=[pl.BlockSpec((tq, d), lambda i: (i, 0)), pl.BlockSpec((m, d), lambda i: (0, 0)),
                  pl.BlockSpec((m, d), lambda i: (0, 0))],
        out_specs=pl.BlockSpec((tq, d), lambda i: (i, 0)),
        out_shape=jax.ShapeDtypeStruct((s, d), BF16),
        compiler_params=_cparams("parallel"),
        name="cross_attention",
    )(q, k, v)


def _peer_scores_kernel(x_ref, wq_ref, keys_ref, o_ref):
    q = jnp.dot(x_ref[...], wq_ref[...], preferred_element_type=F32).astype(BF16)
    nblk, nk, dk = keys_ref.shape
    for b in range(nblk):
        o_ref[b * nk:(b + 1) * nk, :] = lax.dot_general(
            keys_ref[b], q[:, b * dk:(b + 1) * dk], (((1,), (1,)), ((), ())), preferred_element_type=F32)


def peer_scores(xb, wq, keys, tm):
    s, d = xb.shape
    nblk, nk, dk = keys.shape
    return pl.pallas_call(
        _peer_scores_kernel,
        grid=(s // tm,),
        in_specs=[pl.BlockSpec((tm, d), lambda i: (i, 0)), pl.BlockSpec(wq.shape, lambda i: (0, 0)),
                  pl.BlockSpec(keys.shape, lambda i: (0, 0, 0))],
        out_specs=pl.BlockSpec((nblk * nk, tm), lambda i: (0, i)),
        out_shape=jax.ShapeDtypeStruct((nblk * nk, s), F32),
        compiler_params=_cparams("parallel"),
        name="peer_scores",
    )(xb, wq, keys)


def _select_round(work, rows, sentinel):
    m = jnp.max(work, axis=0, keepdims=True)
    idx = jnp.min(jnp.where(work == m, rows, sentinel), axis=0, keepdims=True)
    hit = rows == idx
    return m, idx, hit, jnp.where(hit, -jnp.inf, work)


def _peer_route_kernel(s_ref, e1_ref, n_ref, rk_ref, e2_ref):
    nk, k = PEER_NKEYS, PEER_TOPK
    tt = s_ref.shape[1]
    rows = lax.broadcasted_iota(jnp.int32, (nk, tt), 0)
    rows_k = lax.broadcasted_iota(jnp.int32, (k, tt), 0)
    rows_c = lax.broadcasted_iota(jnp.int32, (k * k, tt), 0)

    def head(h, _):
        s0 = s_ref[pl.ds(pl.multiple_of(h * 2 * nk, nk), nk), :]
        s1 = s_ref[pl.ds(pl.multiple_of(h * 2 * nk + nk, nk), nk), :]

        def stage1(r, carry):
            w0, ts0, ti0, w1, ts1, rk1 = carry
            m0, i0, _, w0 = _select_round(w0, rows, nk)
            m1, _, hit1, w1 = _select_round(w1, rows, nk)
            cur = rows_k == r
            return (w0, jnp.where(cur, m0, ts0), jnp.where(cur, i0, ti0),
                    w1, jnp.where(cur, m1, ts1), jnp.where(hit1, r, rk1))

        zk = jnp.zeros((k, tt), F32)
        w0, ts0, ti0, w1, ts1, rk1 = lax.fori_loop(
            0, k, stage1, (s0, zk, jnp.zeros((k, tt), jnp.int32), s1, zk, jnp.full((nk, tt), k, jnp.int32)))

        cand = jnp.concatenate([ts0[i:i + 1, :] + ts1 for i in range(k)], axis=0)

        def stage2(r, carry):
            wc, best = carry
            m, _, _, wc = _select_round(wc, rows_c, k * k)
            return wc, jnp.where(rows_k == r, m, best)

        wc, best = lax.fori_loop(0, k, stage2, (cand, zk))
        taken = jnp.where(wc == -jnp.inf, 1.0, 0.0)
        n_key = jnp.zeros((nk, tt), F32)
        for i in range(k):
            n_i = jnp.sum(taken[i * k:(i + 1) * k, :], axis=0, keepdims=True)
            n_key = jnp.where(rows == ti0[i:i + 1, :], n_i, n_key)
        norm = jnp.sum(jnp.exp(best - best[0:1, :]), axis=0, keepdims=True)
        out = pl.ds(pl.multiple_of(h * nk, nk), nk)
        e1_ref[out, :] = jnp.where(w0 == -jnp.inf, jnp.exp(s0 - ts0[0:1, :]), 0.0)
        n_ref[out, :] = n_key
        rk_ref[out, :] = rk1.astype(F32)
        e2_ref[out, :] = jnp.where(rk1 < k, jnp.exp(s1 - ts1[0:1, :]), 0.0) / norm
        return 0

    lax.fori_loop(0, PEER_HEADS, head, 0)


def peer_route(scores_t, tt):
    r, s = scores_t.shape
    hk = r // 2
    out = pl.BlockSpec((hk, tt), lambda i: (0, i))
    return pl.pallas_call(
        _peer_route_kernel,
        grid=(s // tt,),
        in_specs=[pl.BlockSpec((r, tt), lambda i: (0, i))],
        out_specs=[out, out, out, out],
        out_shape=[jax.ShapeDtypeStruct((hk, s), F32)] * 4,
        compiler_params=_cparams("parallel"),
        name="peer_route",
    )(scores_t)


def _peer_dense_kernel(x_ref, u_ref, vt_ref, e1_ref, n_ref, rk_ref, e2_ref, o_ref):
    @pl.when(pl.program_id(1) == 0)
    def _():
        o_ref[...] = jnp.zeros_like(o_ref)

    te = u_ref.shape[0]
    na = te // PEER_NKEYS
    a0 = pl.program_id(1) * na
    ht = lax.dot_general(u_ref[...], x_ref[...], (((1,), (1,)), ((), ())), preferred_element_type=F32)
    acts = []
    for ai in range(na):
        w = None
        for hd in range(PEER_HEADS):
            rows = slice(hd * PEER_NKEYS, (hd + 1) * PEER_NKEYS)
            row = hd * PEER_NKEYS + a0 + ai
            n = n_ref[pl.ds(row, 1), :]
            e1 = e1_ref[pl.ds(row, 1), :]
            term = jnp.where(rk_ref[rows, :] < n, e2_ref[rows, :], 0.0) * e1
            w = term if w is None else w + term
        acts.append((_gelu(ht[ai * PEER_NKEYS:(ai + 1) * PEER_NKEYS, :]) * w).astype(BF16))
    act = jnp.concatenate(acts, axis=0)
    o_ref[...] += jnp.dot(vt_ref[...], act, preferred_element_type=F32)


def peer_dense(xb, u_tab, vt_tab, e1t, nt, rkt, e2t, tt, te):
    s, d = xb.shape
    ne = u_tab.shape[0]
    hk = e1t.shape[0]
    side = pl.BlockSpec((hk, tt), lambda i, j: (0, i))
    return pl.pallas_call(
        _peer_dense_kernel,
        grid=(s // tt, ne // te),
        in_specs=[pl.BlockSpec((tt, d), lambda i, j: (i, 0)), pl.BlockSpec((te, d), lambda i, j: (j, 0)),
                  pl.BlockSpec((d, te), lambda i, j: (0, j)), side, side, side, side],
        out_specs=pl.BlockSpec((d, tt), lambda i, j: (0, i)),
        out_shape=jax.ShapeDtypeStruct((d, s), F32),
        compiler_params=_cparams("parallel", "arbitrary"),
        name="peer_dense",
    )(xb, u_tab, vt_tab, e1t, nt, rkt, e2t)


def _res_ln_t_kernel(ht_ref, x_ref, g_ref, b_ref, o_ref, ob_ref):
    y = _layer_norm(DN_ALPHA * x_ref[...] + ht_ref[...].T, g_ref[...], b_ref[...])
    o_ref[...] = y
    ob_ref[...] = y.astype(BF16)


def res_ln_t(ht, x, g, b, tm, name):
    m, n = x.shape
    row = lambda i: (i, 0)
    fixed = lambda i: (0, 0)
    return pl.pallas_call(
        _res_ln_t_kernel,
        grid=(m // tm,),
        in_specs=[pl.BlockSpec((n, tm), lambda i: (0, i)), pl.BlockSpec((tm, n), row),
                  pl.BlockSpec((1, n), fixed), pl.BlockSpec((1, n), fixed)],
        out_specs=[pl.BlockSpec((tm, n), row), pl.BlockSpec((tm, n), row)],
        out_shape=[jax.ShapeDtypeStruct((m, n), F32), jax.ShapeDtypeStruct((m, n), BF16)],
        compiler_params=_cparams("parallel"),
        name=name,
    )(ht, x, g.reshape(1, n), b.reshape(1, n))


def _layer(x, xb, memb, p):
    s, d = x.shape
    half = d // 2
    w_in = p['w_in']
    n_main = 7 * half
    n_small = 2 * GDN_HEADS
    w_cat = jnp.concatenate([w_in[:, :n_main], w_in[:, n_main + n_small:], w_in[:, n_main:n_main + n_small],
                             jnp.zeros((d, 128 - n_small), w_in.dtype)], axis=1).astype(BF16)
    proj = mm(xb, w_cat, F32, 1024, 640, "in_proj")
    gate_col0 = n_main
    small0 = n_main + N_BRANCH * d

    g_s5 = half // S5_GROUP
    nchunk = s // S5_CHUNK
    u_r = (proj[:, :half].astype(BF16).reshape(nchunk, S5_CHUNK, g_s5, S5_GROUP)
           .transpose(2, 0, 1, 3).reshape(g_s5, nchunk, S5_CHUNK * S5_GROUP))
    prep = s5_prepare(p['s5_a_re'], p['s5_a_im'], p['s5_log_dt'], p['s5_b_re'], p['s5_b_im'],
                      p['s5_c_re'], p['s5_c_im'], p['s5_d'])
    y_r = s5_branch(u_r, *prep)
    y_s5 = (y_r.reshape(g_s5, nchunk, S5_CHUNK, S5_GROUP).transpose(1, 2, 0, 3).reshape(s, half))
    y_a = glu(y_s5, p['s5_w_glu'].astype(BF16), p['s5_b_glu'].astype(F32), 512)

    y_b = lru_branch(proj, 1, 2, p['lru_conv_w'], p['lru_conv_b'], p['lru_w_a'], p['lru_b_a'],
                     p['lru_w_x'], p['lru_b_x'], p['lru_lambda'], 256)

    y_c = gdn_branch_pallas(proj, 1, 6, small0 // 128, p['gdn_conv_w'], p['gdn_a_log'], p['gdn_dt_bias'],
                            p['gdn_norm_g'], 256)

    merged = merge_branches(y_a, y_b, y_c, p['w_branch'].astype(BF16), proj, gate_col0, 1024, 512)
    x, xb = mm_res_ln(merged, p['w_out'].astype(BF16), x, p['ln_mix_g'], p['ln_mix_b'], 512, "out_proj_ln")

    q = mm(xb, p['xa_wq'].astype(BF16), BF16, 1024, 512, "xa_q")
    k = mm(memb, p['xa_wk'].astype(BF16), BF16, memb.shape[0], 512, "xa_k")
    v = mm(memb, p['xa_wv'].astype(BF16), BF16, memb.shape[0], 512, "xa_v")
    o = cross_attention(q, k, v, 512)
    x, xb = mm_res_ln(o, p['xa_wo'].astype(BF16), x, p['ln_xa_g'], p['ln_xa_b'], 512, "xa_out_ln")

    keys = p['peer_keys']
    keys = keys.reshape(PEER_HEADS * 2, PEER_NKEYS, keys.shape[-1]).astype(BF16)
    scores_t = peer_scores(xb, p['peer_wq'].astype(BF16), keys, 512)
    e1, n, rk, e2 = peer_route(scores_t, 128)
    ht = peer_dense(xb, p['peer_u'].astype(BF16), p['peer_v'].T.astype(BF16), e1, n, rk, e2, 512, 512)
    return res_ln_t(ht, x, p['ln_ffn_g'], p['ln_ffn_b'], 512, "ffn_ln")


def kernel(x, mem, ln_mix_g, ln_mix_b, w_in, s5_a_re, s5_a_im, s5_log_dt, s5_b_re, s5_b_im, s5_c_re, s5_c_im, s5_d, s5_w_glu, s5_b_glu, lru_conv_w, lru_conv_b, lru_w_a, lru_b_a, lru_w_x, lru_b_x, lru_lambda, gdn_conv_w, gdn_a_log, gdn_dt_bias, gdn_norm_g, w_branch, w_out, ln_xa_g, ln_xa_b, xa_wq, xa_wk, xa_wv, xa_wo, ln_ffn_g, ln_ffn_b, peer_wq, peer_keys, peer_u, peer_v):
    params = dict(ln_mix_g=ln_mix_g, ln_mix_b=ln_mix_b, w_in=w_in, s5_a_re=s5_a_re, s5_a_im=s5_a_im,
                  s5_log_dt=s5_log_dt, s5_b_re=s5_b_re, s5_b_im=s5_b_im, s5_c_re=s5_c_re, s5_c_im=s5_c_im,
                  s5_d=s5_d, s5_w_glu=s5_w_glu, s5_b_glu=s5_b_glu, lru_conv_w=lru_conv_w, lru_conv_b=lru_conv_b,
                  lru_w_a=lru_w_a, lru_b_a=lru_b_a, lru_w_x=lru_w_x, lru_b_x=lru_b_x, lru_lambda=lru_lambda,
                  gdn_conv_w=gdn_conv_w, gdn_a_log=gdn_a_log, gdn_dt_bias=gdn_dt_bias, gdn_norm_g=gdn_norm_g,
                  w_branch=w_branch, w_out=w_out, ln_xa_g=ln_xa_g, ln_xa_b=ln_xa_b, xa_wq=xa_wq, xa_wk=xa_wk,
                  xa_wv=xa_wv, xa_wo=xa_wo, ln_ffn_g=ln_ffn_g, ln_ffn_b=ln_ffn_b, peer_wq=peer_wq,
                  peer_keys=peer_keys, peer_u=peer_u, peer_v=peer_v)
    bsz, seq, d = x.shape
    outs = []
    for b in range(bsz):
        xf = x[b].astype(F32)
        xb = xf.astype(BF16)
        memb = mem[b].astype(BF16)
        for l in range(DEPTH):
            xf, xb = _layer(xf, xb, memb, {k: v[l] for k, v in params.items()})
        outs.append(xf)
    return jnp.stack(outs, axis=0)
```

```python
import functools
import math

import jax
import jax.numpy as jnp
from jax import lax
from jax.experimental import pallas as pl
from jax.experimental.pallas import tpu as pltpu

F32 = jnp.float32
BF16 = jnp.bfloat16

S5_GROUP = 16
S5_STATE = 64
S5_CHUNK = 16
LRU_BLOCKS = 8
LRU_C = 8.0
CONV_K = 4
GDN_HEADS = 8
GDN_DK = 128
GDN_DV = 128
GDN_CHUNK = 64
N_BRANCH = 3
XA_HEADS = 4
PEER_HEADS = 8
PEER_NKEYS = 128
PEER_TOPK = 16
PEER_SUB = 512
DEPTH = 2
DN_ALPHA = (2 * DEPTH) ** 0.25
LN_EPS = 1e-5
NEG_BIG = -1e30
POS_BIG = 1e30
VMEM_LIMIT = 56 * 1024 * 1024


def _cparams(*sem):
    return pltpu.CompilerParams(dimension_semantics=sem, vmem_limit_bytes=VMEM_LIMIT)


def _gelu(x):
    return 0.5 * x * (1.0 + jnp.tanh(math.sqrt(2.0 / math.pi) * (x + 0.044715 * (x * x * x))))


def _sigmoid(x):
    return 1.0 / (1.0 + jnp.exp(-x))


def _softplus(x):
    return jnp.maximum(x, 0.0) + jnp.log1p(jnp.exp(-jnp.abs(x)))


def _layer_norm(y, g, b):
    mu = jnp.mean(y, axis=-1, keepdims=True)
    d = y - mu
    var = jnp.mean(d * d, axis=-1, keepdims=True)
    return d * lax.rsqrt(var + LN_EPS) * g + b


def _mm_kernel(a_ref, b_ref, o_ref):
    o_ref[...] = jnp.dot(a_ref[...], b_ref[...], preferred_element_type=F32).astype(o_ref.dtype)


def mm(a, b, out_dtype, tm, tn, name):
    m, k = a.shape
    n = b.shape[1]
    return pl.pallas_call(
        _mm_kernel,
        grid=(m // tm, n // tn),
        in_specs=[pl.BlockSpec((tm, k), lambda i, j: (i, 0)),
                  pl.BlockSpec((k, tn), lambda i, j: (0, j))],
        out_specs=pl.BlockSpec((tm, tn), lambda i, j: (i, j)),
        out_shape=jax.ShapeDtypeStruct((m, n), out_dtype),
        compiler_params=_cparams("parallel", "parallel"),
        name=name,
    )(a, b)


def _mm_res_ln_kernel(a_ref, w_ref, x_ref, g_ref, b_ref, o_ref, ob_ref):
    h = jnp.dot(a_ref[...], w_ref[...], preferred_element_type=F32)
    y = _layer_norm(DN_ALPHA * x_ref[...] + h, g_ref[...], b_ref[...])
    o_ref[...] = y
    ob_ref[...] = y.astype(BF16)


def mm_res_ln(a, w, x, g, b, tm, name):
    m, k = a.shape
    n = w.shape[1]
    row = lambda i: (i, 0)
    fixed = lambda i: (0, 0)
    return pl.pallas_call(
        _mm_res_ln_kernel,
        grid=(m // tm,),
        in_specs=[pl.BlockSpec((tm, k), row), pl.BlockSpec((k, n), fixed), pl.BlockSpec((tm, n), row),
                  pl.BlockSpec((1, n), fixed), pl.BlockSpec((1, n), fixed)],
        out_specs=[pl.BlockSpec((tm, n), row), pl.BlockSpec((tm, n), row)],
        out_shape=[jax.ShapeDtypeStruct((m, n), F32), jax.ShapeDtypeStruct((m, n), BF16)],
        compiler_params=_cparams("parallel"),
        name=name,
    )(a, w, x, g.reshape(1, n), b.reshape(1, n))


def _res_ln_kernel(h_ref, x_ref, g_ref, b_ref, o_ref, ob_ref):
    y = _layer_norm(DN_ALPHA * x_ref[...] + h_ref[...], g_ref[...], b_ref[...])
    o_ref[...] = y
    ob_ref[...] = y.astype(BF16)


def res_ln(h, x, g, b, tm, name):
    m, n = x.shape
    row = lambda i: (i, 0)
    fixed = lambda i: (0, 0)
    return pl.pallas_call(
        _res_ln_kernel,
        grid=(m // tm,),
        in_specs=[pl.BlockSpec((tm, n), row), pl.BlockSpec((tm, n), row),
                  pl.BlockSpec((1, n), fixed), pl.BlockSpec((1, n), fixed)],
        out_specs=[pl.BlockSpec((tm, n), row), pl.BlockSpec((tm, n), row)],
        out_shape=[jax.ShapeDtypeStruct((m, n), F32), jax.ShapeDtypeStruct((m, n), BF16)],
        compiler_params=_cparams("parallel"),
        name=name,
    )(h, x, g.reshape(1, n), b.reshape(1, n))


def s5_prepare(a_re, a_im, log_dt, b_re, b_im, c_re, c_im, d_skip):
    hi = lax.Precision.HIGHEST
    L = S5_CHUNK
    lam_re, lam_im = a_re.astype(F32), a_im.astype(F32)
    dt = jnp.exp(log_dt.astype(F32))[:, None]
    mag = jnp.exp(lam_re * dt)
    abar_re, abar_im = mag * jnp.cos(lam_im * dt), mag * jnp.sin(lam_im * dt)
    den = lam_re * lam_re + lam_im * lam_im
    num_re = abar_re - 1.0
    f_re = (num_re * lam_re + abar_im * lam_im) / den
    f_im = (abar_im * lam_re - num_re * lam_im) / den
    bre, bim = b_re.astype(F32), b_im.astype(F32)
    bbar_re = f_re[..., None] * bre - f_im[..., None] * bim
    bbar_im = f_re[..., None] * bim + f_im[..., None] * bre
    steps = jnp.arange(L + 1, dtype=F32)[:, None, None]
    pw_mag = jnp.exp(lam_re * dt * steps)
    pw_re = pw_mag * jnp.cos(lam_im * dt * steps)
    pw_im = pw_mag * jnp.sin(lam_im * dt * steps)
    cre, cim = c_re.astype(F32), c_im.astype(F32)
    ca_re = cre[None] * pw_re[:, :, None, :] - cim[None] * pw_im[:, :, None, :]
    ca_im = cre[None] * pw_im[:, :, None, :] + cim[None] * pw_re[:, :, None, :]
    kern = (jnp.einsum('dgpn,gnq->dgpq', ca_re[:L], bbar_re, precision=hi)
            - jnp.einsum('dgpn,gnq->dgpq', ca_im[:L], bbar_im, precision=hi))
    kern = kern.at[0].add(d_skip.astype(F32)[:, :, None] * jnp.eye(S5_GROUP, dtype=F32)[None])
    lag = jnp.arange(L)[None, :] - jnp.arange(L)[:, None]
    t5 = jnp.where((lag >= 0)[:, :, None, None, None], kern[jnp.clip(lag, 0, L - 1)], 0.0)
    g = a_re.shape[0]
    t_mat = t5.transpose(2, 0, 4, 1, 3).reshape(g, L * S5_GROUP, L * S5_GROUP)
    rev_re, rev_im = pw_re[L - 1::-1][:L], pw_im[L - 1::-1][:L]
    bc_re = rev_re[..., None] * bbar_re[None] - rev_im[..., None] * bbar_im[None]
    bc_im = rev_re[..., None] * bbar_im[None] + rev_im[..., None] * bbar_re[None]
    bc_mat = jnp.concatenate([bc_re.transpose(1, 0, 3, 2), bc_im.transpose(1, 0, 3, 2)],
                             axis=-1).reshape(g, L * S5_GROUP, 2 * S5_STATE)
    cc_re = ca_re[1:].transpose(1, 3, 0, 2)
    cc_im = -ca_im[1:].transpose(1, 3, 0, 2)
    cc_mat = jnp.concatenate([cc_re, cc_im], axis=1).reshape(g, 2 * S5_STATE, L * S5_GROUP)
    a1 = jnp.concatenate([pw_re[L], pw_re[L]], axis=-1)
    a2 = jnp.concatenate([-pw_im[L], pw_im[L]], axis=-1)
    return t_mat.astype(BF16), bc_mat.astype(BF16), cc_mat.astype(BF16), a1, a2


def _s5_z_kernel(u_ref, bc_ref, z_ref):
    z_ref[...] = jnp.dot(u_ref[0], bc_ref[0], preferred_element_type=F32)


def _s5_scan_kernel(z_ref, a1_ref, a2_ref, s_ref, carry_ref):
    @pl.when(pl.program_id(0) == 0)
    def _():
        carry_ref[...] = jnp.zeros_like(carry_ref)

    a1 = a1_ref[...]
    a2 = a2_ref[...]
    width = z_ref.shape[1]
    lane = lax.broadcasted_iota(jnp.int32, (1, width), 1)
    real_half = (lane & (2 * S5_STATE - 1)) < S5_STATE

    def body(c, state):
        s_ref[pl.ds(c, 1), :] = state
        swapped = jnp.where(real_half, pltpu.roll(state, width - S5_STATE, axis=1),
                            pltpu.roll(state, S5_STATE, axis=1))
        return a1 * state + a2 * swapped + z_ref[pl.ds(c, 1), :]

    carry_ref[...] = lax.fori_loop(0, z_ref.shape[0], body, carry_ref[...])


def _s5_out_kernel(u_ref, t_ref, s_ref, cc_ref, y_ref):
    y = jnp.dot(u_ref[0], t_ref[0], preferred_element_type=F32)
    y = y + jnp.dot(s_ref[...].astype(BF16), cc_ref[0], preferred_element_type=F32)
    y_ref[0] = _gelu(y)


def s5_branch(u_r, t_mat, bc_mat, cc_mat, a1, a2):
    g, c, w = u_r.shape
    ns = 2 * S5_STATE
    z = pl.pallas_call(
        _s5_z_kernel,
        grid=(g,),
        in_specs=[pl.BlockSpec((1, c, w), lambda i: (i, 0, 0)), pl.BlockSpec((1, w, ns), lambda i: (i, 0, 0))],
        out_specs=pl.BlockSpec((c, ns), lambda i: (0, i)),
        out_shape=jax.ShapeDtypeStruct((c, g * ns), F32),
        compiler_params=_cparams("parallel"),
        name="s5_chunk_inputs",
    )(u_r, bc_mat)
    cb = min(64, c)
    s_prev = pl.pallas_call(
        _s5_scan_kernel,
        grid=(c // cb,),
        in_specs=[pl.BlockSpec((cb, g * ns), lambda i: (i, 0)),
                  pl.BlockSpec((1, g * ns), lambda i: (0, 0)), pl.BlockSpec((1, g * ns), lambda i: (0, 0))],
        out_specs=pl.BlockSpec((cb, g * ns), lambda i: (i, 0)),
        out_shape=jax.ShapeDtypeStruct((c, g * ns), F32),
        scratch_shapes=[pltpu.VMEM((1, g * ns), F32)],
        compiler_params=_cparams("arbitrary"),
        name="s5_chunk_scan",
    )(z, a1.reshape(1, g * ns), a2.reshape(1, g * ns))
    return pl.pallas_call(
        _s5_out_kernel,
        grid=(g,),
        in_specs=[pl.BlockSpec((1, c, w), lambda i: (i, 0, 0)), pl.BlockSpec((1, w, w), lambda i: (i, 0, 0)),
                  pl.BlockSpec((c, ns), lambda i: (0, i)), pl.BlockSpec((1, ns, w), lambda i: (i, 0, 0))],
        out_specs=pl.BlockSpec((1, c, w), lambda i: (i, 0, 0)),
        out_shape=jax.ShapeDtypeStruct((g, c, w), F32),
        compiler_params=_cparams("parallel"),
        name="s5_outputs",
    )(u_r, t_mat, s_prev, cc_mat)


def _glu_kernel(y_ref, w_ref, b_ref, o_ref):
    y = y_ref[...]
    gate = jnp.dot(y.astype(BF16), w_ref[...], preferred_element_type=F32) + b_ref[...]
    o_ref[...] = (y * _sigmoid(gate)).astype(o_ref.dtype)


def glu(y, w, b, tm):
    m, n = y.shape
    return pl.pallas_call(
        _glu_kernel,
        grid=(m // tm,),
        in_specs=[pl.BlockSpec((tm, n), lambda i: (i, 0)), pl.BlockSpec((n, n), lambda i: (0, 0)),
                  pl.BlockSpec((1, n), lambda i: (0, 0))],
        out_specs=pl.BlockSpec((tm, n), lambda i: (i, 0)),
        out_shape=jax.ShapeDtypeStruct((m, n), BF16),
        compiler_params=_cparams("parallel"),
        name="s5_glu",
    )(y, w, b.reshape(1, n))


def _lru_kernel(x_ref, gate_ref, cw_ref, cb_ref, wa_ref, ba_ref, wx_ref, bx_ref, lam_ref, o_ref,
                tail_ref, h_ref, a_ref, b_ref):
    tb, width = x_ref.shape
    blk = width // LRU_BLOCKS

    @pl.when(pl.program_id(0) == 0)
    def _():
        tail_ref[...] = jnp.zeros_like(tail_ref)
        h_ref[...] = jnp.zeros_like(h_ref)

    x = x_ref[...]
    ext = jnp.concatenate([tail_ref[...], x], axis=0)
    tail_ref[...] = x[tb - 8:, :]
    xc = cb_ref[...] + sum(cw_ref[j:j + 1, :] * ext[8 - (CONV_K - 1) + j: 8 - (CONV_K - 1) + j + tb, :]
                           for j in range(CONV_K))
    xcb = xc.astype(BF16)
    r = jnp.concatenate([jnp.dot(xcb[:, i * blk:(i + 1) * blk], wa_ref[i], preferred_element_type=F32)
                         for i in range(LRU_BLOCKS)], axis=1)
    gi = jnp.concatenate([jnp.dot(xcb[:, i * blk:(i + 1) * blk], wx_ref[i], preferred_element_type=F32)
                          for i in range(LRU_BLOCKS)], axis=1)
    r = _sigmoid(r + ba_ref[...])
    gi = _sigmoid(gi + bx_ref[...])
    log_a = -LRU_C * r * _softplus(-lam_ref[...])
    a = jnp.exp(log_a)
    a_ref[...] = a
    b_ref[...] = jnp.sqrt(-jnp.tanh(log_a) * (a * a + 1.0)) * (gi * xc)

    def body(t, h):
        h = a_ref[pl.ds(t, 1), :] * h + b_ref[pl.ds(t, 1), :]
        b_ref[pl.ds(t, 1), :] = h
        return h

    h_ref[...] = lax.fori_loop(0, tb, body, h_ref[...], unroll=8)
    o_ref[...] = (b_ref[...] * _gelu(gate_ref[...])).astype(o_ref.dtype)


def lru_branch(proj, x_col, gate_col, conv_w, conv_b, w_a, b_a, w_x, b_x, lam, tb):
    s = proj.shape[0]
    width = conv_w.shape[1]
    fixed2 = lambda i: (0, 0)
    fixed3 = lambda i: (0, 0, 0)
    vec = lambda v: v.reshape(1, width).astype(F32)
    return pl.pallas_call(
        _lru_kernel,
        grid=(s // tb,),
        in_specs=[pl.BlockSpec((tb, width), lambda i: (i, x_col)), pl.BlockSpec((tb, width), lambda i: (i, gate_col)),
                  pl.BlockSpec((CONV_K, width), fixed2), pl.BlockSpec((1, width), fixed2),
                  pl.BlockSpec(w_a.shape, fixed3), pl.BlockSpec((1, width), fixed2),
                  pl.BlockSpec(w_x.shape, fixed3), pl.BlockSpec((1, width), fixed2),
                  pl.BlockSpec((1, width), fixed2)],
        out_specs=pl.BlockSpec((tb, width), lambda i: (i, 0)),
        out_shape=jax.ShapeDtypeStruct((s, width), BF16),
        scratch_shapes=[pltpu.VMEM((8, width), F32), pltpu.VMEM((1, width), F32),
                        pltpu.VMEM((tb, width), F32), pltpu.VMEM((tb, width), F32)],
        compiler_params=_cparams("arbitrary"),
        name="rglru",
    )(proj, proj, conv_w.astype(F32), vec(conv_b), w_a.astype(BF16), vec(b_a), w_x.astype(BF16), vec(b_x), vec(lam))


def _split3(x):
    hi = x.astype(BF16)
    r1 = x - hi.astype(F32)
    mid = r1.astype(BF16)
    lo = (r1 - mid.astype(F32)).astype(BF16)
    return hi, mid, lo


def _nt(a, b):
    return lax.dot_general(a, b, (((1,), (1,)), ((), ())), preferred_element_type=F32)


def _gdn_kernel(qkv_ref, og_ref, ba_ref, cw_ref, alog_ref, dtb_ref, ng_ref, o_ref,
                tail_ref, state_ref, q_s, k_s, v_s, beta_s, g_s):
    tb = qkv_ref.shape[0]
    c = GDN_CHUNK
    dk, dv = GDN_DK, GDN_DV
    nqk = GDN_HEADS * dk

    @pl.when(pl.program_id(0) == 0)
    def _():
        tail_ref[...] = jnp.zeros_like(tail_ref)
        state_ref[...] = jnp.zeros_like(state_ref)

    x = qkv_ref[...]
    ext = jnp.concatenate([tail_ref[...], x], axis=0)
    tail_ref[...] = x[tb - 8:, :]
    xc = sum(cw_ref[j:j + 1, :] * ext[8 - (CONV_K - 1) + j: 8 - (CONV_K - 1) + j + tb, :] for j in range(CONV_K))
    xc = xc * _sigmoid(xc)
    for h in range(GDN_HEADS):
        qh = xc[:, h * dk:(h + 1) * dk]
        kh = xc[:, nqk + h * dk: nqk + (h + 1) * dk]
        q_s[:, h * dk:(h + 1) * dk] = qh * lax.rsqrt(jnp.sum(qh * qh, axis=-1, keepdims=True) + 1e-6) * dk ** -0.5
        k_s[:, h * dk:(h + 1) * dk] = kh * lax.rsqrt(jnp.sum(kh * kh, axis=-1, keepdims=True) + 1e-6)
    v_s[...] = xc[:, 2 * nqk:]
    ba = ba_ref[...]
    beta_s[...] = _sigmoid(ba)
    g_s[...] = -jnp.exp(alog_ref[...]) * _softplus(ba + dtb_ref[...])

    ri = lax.broadcasted_iota(jnp.int32, (c, c), 0)
    ci = lax.broadcasted_iota(jnp.int32, (c, c), 1)
    causal = ri >= ci
    strict = ri > ci
    tril = jnp.where(causal, 1.0, 0.0).astype(BF16)
    eye = jnp.where(ri == ci, 1.0, 0.0)
    lane = lax.broadcasted_iota(jnp.int32, (c, 128), 1)
    pad_x = jnp.where((lane >= 3) & (lane < 6), 1.0, 0.0)
    pad_y = jnp.where(lane < 3, 1.0, 0.0)

    def chunk(ic, _):
        r0 = pl.multiple_of(ic * c, c)
        rows = pl.ds(r0, c)
        g_parts = _split3(g_s[rows, :])
        gcum = sum(jnp.dot(tril, part, preferred_element_type=F32) for part in g_parts)
        beta = beta_s[rows, :]
        heads = range(GDN_HEADS)
        hs = [slice(h * dk, (h + 1) * dk) for h in heads]
        dot = functools.partial(jnp.dot, preferred_element_type=F32)
        gcol = [gcum[:, GDN_HEADS + h:GDN_HEADS + h + 1] for h in heads]
        q = [q_s[rows, hs[h]] for h in heads]
        k = [k_s[rows, hs[h]] for h in heads]
        kbf = [k[h].astype(BF16) for h in heads]
        kb = [k[h] * beta[:, h:h + 1] for h in heads]
        vb = [(v_s[rows, hs[h]] * beta[:, h:h + 1]).astype(BF16) for h in heads]
        decay = []
        for h in heads:
            hi, mid, lo = (p.astype(F32) for p in _split3(gcol[h]))
            xa = jnp.where(lane == 0, hi, jnp.where(lane == 1, mid, jnp.where(lane == 2, lo, pad_x)))
            ya = jnp.where(lane == 3, -hi, jnp.where(lane == 4, -mid, jnp.where(lane == 5, -lo, pad_y)))
            decay.append(jnp.exp(jnp.where(causal, _nt(xa.astype(BF16), ya.astype(BF16)), -jnp.inf)))
        neg_a = [jnp.where(strict, -_nt(kb[h].astype(BF16), kbf[h]) * decay[h], 0.0) for h in heads]
        intra = [jnp.where(causal, _nt(q[h].astype(BF16), kbf[h]) * decay[h], 0.0).astype(BF16) for h in heads]
        t_off = list(neg_a)
        pw = list(neg_a)
        for _ in range(5):
            pwb = [pw[h].astype(BF16) for h in heads]
            pw = [dot(pwb[h], pwb[h]) for h in heads]
            t_off = [t_off[h] + pw[h] + dot(t_off[h].astype(BF16), pw[h].astype(BF16)) for h in heads]
        t_inv = [(eye + t_off[h]).astype(BF16) for h in heads]
        eg = [jnp.exp(gcol[h]) for h in heads]
        w = [dot(t_inv[h], (kb[h] * eg[h]).astype(BF16)).astype(BF16) for h in heads]
        u = [dot(t_inv[h], vb[h]) for h in heads]
        state = [state_ref[h] for h in heads]
        sb = [state[h].astype(BF16) for h in heads]
        vnb = [(u[h] - dot(w[h], sb[h])).astype(BF16) for h in heads]
        g_last = [gcol[h][c - 1:c, :] for h in heads]
        k_dec = [(k[h] * jnp.exp(g_last[h] - gcol[h])).astype(BF16) for h in heads]
        for h in heads:
            state_ref[h] = state[h] * jnp.exp(g_last[h]) + lax.dot_general(
                k_dec[h], vnb[h], (((0,), (0,)), ((), ())), preferred_element_type=F32)
        o = [dot((q[h] * eg[h]).astype(BF16), sb[h]) + dot(intra[h], vnb[h]) for h in heads]
        for h in heads:
            on = o[h] * lax.rsqrt(jnp.mean(o[h] * o[h], axis=-1, keepdims=True) + 1e-6) * ng_ref[...]
            og = og_ref[rows, hs[h]]
            o_ref[rows, hs[h]] = (on * (og * _sigmoid(og))).astype(o_ref.dtype)
        return 0

    lax.fori_loop(0, tb // c, chunk, 0)


def gdn_branch_pallas(proj, qkv_col, og_col, ba_col, conv_w, a_log, dt_bias, norm_g, tb):
    s = proj.shape[0]
    nv = GDN_HEADS * GDN_DV
    nqkv = conv_w.shape[1]
    pad = jnp.zeros((GDN_HEADS,), F32)
    lanes = lambda v: jnp.concatenate([pad, v.astype(F32), jnp.zeros((128 - 2 * GDN_HEADS,), F32)]).reshape(1, 128)
    fixed = lambda i: (0, 0)
    return pl.pallas_call(
        _gdn_kernel,
        grid=(s // tb,),
        in_specs=[pl.BlockSpec((tb, nqkv), lambda i: (i, qkv_col)), pl.BlockSpec((tb, nv), lambda i: (i, og_col)),
                  pl.BlockSpec((tb, 128), lambda i: (i, ba_col)), pl.BlockSpec((CONV_K, nqkv), fixed),
                  pl.BlockSpec((1, 128), fixed), pl.BlockSpec((1, 128), fixed), pl.BlockSpec((1, GDN_DV), fixed)],
        out_specs=pl.BlockSpec((tb, nv), lambda i: (i, 0)),
        out_shape=jax.ShapeDtypeStruct((s, nv), BF16),
        scratch_shapes=[pltpu.VMEM((8, nqkv), F32), pltpu.VMEM((GDN_HEADS, GDN_DK, GDN_DV), F32),
                        pltpu.VMEM((tb, nv), F32), pltpu.VMEM((tb, nv), F32), pltpu.VMEM((tb, nv), F32),
                        pltpu.VMEM((tb, 128), F32), pltpu.VMEM((tb, 128), F32)],
        compiler_params=_cparams("arbitrary"),
        name="gated_deltanet",
    )(proj, proj, proj, conv_w.astype(F32), lanes(a_log), lanes(dt_bias), norm_g.astype(F32).reshape(1, GDN_DV))


def _merge_kernel(ya_ref, yb_ref, yc_ref, wb_ref, ga_ref, gb_ref, gc_ref, o_ref):
    acc = None
    for n, (y_ref, g_ref) in enumerate(((ya_ref, ga_ref), (yb_ref, gb_ref), (yc_ref, gc_ref))):
        d = jnp.dot(y_ref[...], wb_ref[n], preferred_element_type=F32) * _sigmoid(g_ref[...])
        acc = d if acc is None else acc + d
    o_ref[...] = acc.astype(o_ref.dtype)


def merge_branches(ya, yb, yc, w_branch, proj, gate_col0, tm, tn):
    s, width = ya.shape
    d = w_branch.shape[2]
    nj = d // tn
    c0 = gate_col0 // tn
    yspec = pl.BlockSpec((tm, width), lambda i, j: (i, 0))
    gspec = lambda n: pl.BlockSpec((tm, tn), lambda i, j: (i, c0 + n * nj + j))
    return pl.pallas_call(
        _merge_kernel,
        grid=(s // tm, nj),
        in_specs=[yspec, yspec, yspec, pl.BlockSpec((N_BRANCH, width, tn), lambda i, j: (0, 0, j)),
                  gspec(0), gspec(1), gspec(2)],
        out_specs=pl.BlockSpec((tm, tn), lambda i, j: (i, j)),
        out_shape=jax.ShapeDtypeStruct((s, d), BF16),
        compiler_params=_cparams("parallel", "parallel"),
        name="merge_branches",
    )(ya, yb, yc, w_branch, proj, proj, proj)


def _xattn_kernel(q_ref, k_ref, v_ref, o_ref):
    d = q_ref.shape[1]
    hd = d // XA_HEADS
    outs = []
    for h in range(XA_HEADS):
        sl = slice(h * hd, (h + 1) * hd)
        s = lax.dot_general(q_ref[:, sl], k_ref[:, sl], (((1,), (1,)), ((), ())),
                            preferred_element_type=F32) * hd ** -0.5
        s = s - jnp.max(s, axis=-1, keepdims=True)
        p = jnp.exp(s)
        p = p / jnp.sum(p, axis=-1, keepdims=True)
        outs.append(jnp.dot(p.astype(BF16), v_ref[:, sl], preferred_element_type=F32))
    o_ref[...] = jnp.concatenate(outs, axis=1).astype(o_ref.dtype)


def cross_attention(q, k, v, tq):
    s, d = q.shape
    m = k.shape[0]
    return pl.pallas_call(
        _xattn_kernel,
        grid=(s // tq,),
        in_specs=[pl.BlockSpec((tq, d), lambda i: (i, 0)), pl.BlockSpec((m, d), lambda i: (0, 0)),
                  pl.BlockSpec((m, d), lambda i: (0, 0))],
        out_specs=pl.BlockSpec((tq, d), lambda i: (i, 0)),
        out_shape=jax.ShapeDtypeStruct((s, d), BF16),
        compiler_params=_cparams("parallel"),
        name="cross_attention",
    )(q, k, v)


def _peer_scores_kernel(x_ref, wq_ref, keys_ref, o_ref):
    q = jnp.dot(x_ref[...], wq_ref[...], preferred_element_type=F32).astype(BF16)
    nblk, nk, dk = keys_ref.shape
    for b in range(nblk):
        o_ref[b * nk:(b + 1) * nk, :] = lax.dot_general(
            keys_ref[b], q[:, b * dk:(b + 1) * dk], (((1,), (1,)), ((), ())), preferred_element_type=F32)


def peer_scores(xb, wq, keys, tm):
    s, d = xb.shape
    nblk, nk, dk = keys.shape
    return pl.pallas_call(
        _peer_scores_kernel,
        grid=(s // tm,),
        in_specs=[pl.BlockSpec((tm, d), lambda i: (i, 0)), pl.BlockSpec(wq.shape, lambda i: (0, 0)),
                  pl.BlockSpec(keys.shape, lambda i: (0, 0, 0))],
        out_specs=pl.BlockSpec((nblk * nk, tm), lambda i: (0, i)),
        out_shape=jax.ShapeDtypeStruct((nblk * nk, s), F32),
        compiler_params=_cparams("parallel"),
        name="peer_scores",
    )(xb, wq, keys)


def _select_round(work, rows, sentinel):
    m = jnp.max(work, axis=0, keepdims=True)
    idx = jnp.min(jnp.where(work == m, rows, sentinel), axis=0, keepdims=True)
    hit = rows == idx
    return m, idx, hit, jnp.where(hit, -jnp.inf, work)


def _peer_route_kernel(s_ref, sel0_ref, sel1_ref, pad_ref, grp_ref, e1_ref, n_ref, rk_ref, e2_ref):
    nk, k = PEER_NKEYS, PEER_TOPK
    tt = s_ref.shape[1]
    nc = sel0_ref.shape[0]
    rows = lax.broadcasted_iota(jnp.int32, (nk, tt), 0)
    rows_k = lax.broadcasted_iota(jnp.int32, (k, tt), 0)
    rows_c = lax.broadcasted_iota(jnp.int32, (nc, tt), 0)

    def head(h, _):
        s0 = s_ref[pl.ds(pl.multiple_of(h * 2 * nk, nk), nk), :]
        s1 = s_ref[pl.ds(pl.multiple_of(h * 2 * nk + nk, nk), nk), :]

        def stage1(r, carry):
            w0, ts0, ti0, w1, ts1, rk1 = carry
            m0, i0, _, w0 = _select_round(w0, rows, nk)
            m1, _, hit1, w1 = _select_round(w1, rows, nk)
            cur = rows_k == r
            return (w0, jnp.where(cur, m0, ts0), jnp.where(cur, i0, ti0),
                    w1, jnp.where(cur, m1, ts1), jnp.where(hit1, r, rk1))

        zk = jnp.zeros((k, tt), F32)
        w0, ts0, ti0, w1, ts1, rk1 = lax.fori_loop(
            0, k, stage1, (s0, zk, jnp.zeros((k, tt), jnp.int32), s1, zk, jnp.full((nk, tt), k, jnp.int32)))

        def pick(sel_ref, vals):
            return sum(jnp.dot(sel_ref[...], part, preferred_element_type=F32) for part in _split3(vals))

        cand = pick(sel0_ref, ts0) + pick(sel1_ref, ts1) + pad_ref[...]

        def stage2(r, carry):
            wc, best = carry
            m, _, _, wc = _select_round(wc, rows_c, nc)
            return wc, jnp.where(rows_k == r, m, best)

        wc, best = lax.fori_loop(0, k, stage2, (cand, zk))
        taken = jnp.where(wc == -jnp.inf, 1.0, 0.0).astype(BF16)
        n_rank = jnp.dot(grp_ref[...], taken, preferred_element_type=F32)
        n_key = jnp.zeros((nk, tt), F32)
        for i in range(k):
            n_key = jnp.where(rows == ti0[i:i + 1, :], n_rank[i:i + 1, :], n_key)
        norm = jnp.sum(jnp.exp(best - best[0:1, :]), axis=0, keepdims=True)
        out = pl.ds(pl.multiple_of(h * nk, nk), nk)
        e1_ref[out, :] = jnp.where(w0 == -jnp.inf, jnp.exp(s0 - ts0[0:1, :]), 0.0)
        n_ref[out, :] = n_key
        rk_ref[out, :] = rk1.astype(F32).astype(BF16)
        e2_ref[out, :] = (jnp.where(rk1 < k, jnp.exp(s1 - ts1[0:1, :]), 0.0) / norm).astype(BF16)
        return 0

    lax.fori_loop(0, PEER_HEADS, head, 0)


def peer_route(scores_t, tt):
    r, s = scores_t.shape
    hk = r // 2
    k = PEER_TOPK
    pairs = [(i, j) for i in range(k) for j in range(k // (i + 1))]
    nc = -(-len(pairs) // 8) * 8
    ranks = jnp.arange(k)[None, :]
    first = jnp.array([p[0] for p in pairs] + [-1] * (nc - len(pairs)))[:, None]
    second = jnp.array([p[1] for p in pairs] + [-1] * (nc - len(pairs)))[:, None]
    sel0 = (first == ranks).astype(BF16)
    sel1 = (second == ranks).astype(BF16)
    pad = jnp.where(first >= 0, 0.0, -jnp.inf).astype(F32)
    fixed = lambda i: (0, 0)
    out = pl.BlockSpec((hk, tt), lambda i: (0, i))
    return pl.pallas_call(
        _peer_route_kernel,
        grid=(s // tt,),
        in_specs=[pl.BlockSpec((r, tt), lambda i: (0, i)), pl.BlockSpec((nc, k), fixed), pl.BlockSpec((nc, k), fixed),
                  pl.BlockSpec((nc, 1), fixed), pl.BlockSpec((k, nc), fixed)],
        out_specs=[out, out, out, out],
        out_shape=[jax.ShapeDtypeStruct((hk, s), F32), jax.ShapeDtypeStruct((hk, s), F32),
                   jax.ShapeDtypeStruct((hk, s), BF16), jax.ShapeDtypeStruct((hk, s), BF16)],
        compiler_params=_cparams("parallel"),
        name="peer_route",
    )(scores_t, sel0, sel1, pad, sel0.T)


def _peer_dense_kernel(x_ref, u_ref, vt_ref, e1_ref, n_ref, rk_ref, e2_ref, o_ref, w_s, h_s):
    @pl.when(pl.program_id(1) == 0)
    def _():
        o_ref[...] = jnp.zeros_like(o_ref)

    te = u_ref.shape[0]
    tt = x_ref.shape[0]
    na = te // PEER_NKEYS
    for ai in range(na):
        w = None
        for hd in range(PEER_HEADS):
            rows = slice(hd * PEER_NKEYS, (hd + 1) * PEER_NKEYS)
            n = jnp.broadcast_to(n_ref[hd, ai:ai + 1, :].astype(BF16), (PEER_NKEYS, tt))
            e1 = jnp.broadcast_to(e1_ref[hd, ai:ai + 1, :].astype(BF16), (PEER_NKEYS, tt))
            term = jnp.where(rk_ref[rows, :] < n, e2_ref[rows, :], jnp.zeros((), BF16)) * e1
            w = term if w is None else w + term
        w_s[ai * PEER_NKEYS:(ai + 1) * PEER_NKEYS, :] = w
    h_s[...] = lax.dot_general(u_ref[...], x_ref[...], (((1,), (1,)), ((), ())), preferred_element_type=F32)
    for sb in range(te // PEER_SUB):
        rows = slice(sb * PEER_SUB, (sb + 1) * PEER_SUB)
        act = _gelu(h_s[rows, :]).astype(BF16) * w_s[rows, :]
        o_ref[...] += jnp.dot(vt_ref[:, rows], act, preferred_element_type=F32)


def peer_dense(xb, u_tab, vt_tab, e1t, nt, rkt, e2t, tt, te):
    s, d = xb.shape
    ne = u_tab.shape[0]
    hk = e1t.shape[0]
    side = pl.BlockSpec((hk, tt), lambda i, j: (0, i))
    na = te // PEER_NKEYS
    first = pl.BlockSpec((PEER_HEADS, na, tt), lambda i, j: (0, j, i))
    e1t = e1t.reshape(PEER_HEADS, PEER_NKEYS, s)
    nt = nt.reshape(PEER_HEADS, PEER_NKEYS, s)
    return pl.pallas_call(
        _peer_dense_kernel,
        grid=(s // tt, ne // te),
        in_specs=[pl.BlockSpec((tt, d), lambda i, j: (i, 0)), pl.BlockSpec((te, d), lambda i, j: (j, 0)),
                  pl.BlockSpec((d, te), lambda i, j: (0, j)), first, first, side, side],
        out_specs=pl.BlockSpec((d, tt), lambda i, j: (0, i)),
        out_shape=jax.ShapeDtypeStruct((d, s), F32),
        scratch_shapes=[pltpu.VMEM((te, tt), BF16), pltpu.VMEM((te, tt), F32)],
        compiler_params=_cparams("parallel", "arbitrary"),
        name="peer_dense",
    )(xb, u_tab, vt_tab, e1t, nt, rkt, e2t)


def _res_ln_t_kernel(ht_ref, x_ref, g_ref, b_ref, o_ref, ob_ref):
    y = _layer_norm(DN_ALPHA * x_ref[...] + ht_ref[...].T, g_ref[...], b_ref[...])
    o_ref[...] = y
    ob_ref[...] = y.astype(BF16)


def res_ln_t(ht, x, g, b, tm, name):
    m, n = x.shape
    row = lambda i: (i, 0)
    fixed = lambda i: (0, 0)
    return pl.pallas_call(
        _res_ln_t_kernel,
        grid=(m // tm,),
        in_specs=[pl.BlockSpec((n, tm), lambda i: (0, i)), pl.BlockSpec((tm, n), row),
                  pl.BlockSpec((1, n), fixed), pl.BlockSpec((1, n), fixed)],
        out_specs=[pl.BlockSpec((tm, n), row), pl.BlockSpec((tm, n), row)],
        out_shape=[jax.ShapeDtypeStruct((m, n), F32), jax.ShapeDtypeStruct((m, n), BF16)],
        compiler_params=_cparams("parallel"),
        name=name,
    )(ht, x, g.reshape(1, n), b.reshape(1, n))


def _layer(x, xb, memb, p):
    s, d = x.shape
    half = d // 2
    w_in = p['w_in']
    n_main = 7 * half
    n_small = 2 * GDN_HEADS
    w_cat = jnp.concatenate([w_in[:, :n_main], w_in[:, n_main + n_small:], w_in[:, n_main:n_main + n_small],
                             jnp.zeros((d, 128 - n_small), w_in.dtype)], axis=1).astype(BF16)
    proj = mm(xb, w_cat, F32, 1024, 640, "in_proj")
    gate_col0 = n_main
    small0 = n_main + N_BRANCH * d

    g_s5 = half // S5_GROUP
    nchunk = s // S5_CHUNK
    u_r = (proj[:, :half].astype(BF16).reshape(nchunk, S5_CHUNK, g_s5, S5_GROUP)
           .transpose(2, 0, 1, 3).reshape(g_s5, nchunk, S5_CHUNK * S5_GROUP))
    prep = s5_prepare(p['s5_a_re'], p['s5_a_im'], p['s5_log_dt'], p['s5_b_re'], p['s5_b_im'],
                      p['s5_c_re'], p['s5_c_im'], p['s5_d'])
    y_r = s5_branch(u_r, *prep)
    y_s5 = (y_r.reshape(g_s5, nchunk, S5_CHUNK, S5_GROUP).transpose(1, 2, 0, 3).reshape(s, half))
    y_a = glu(y_s5, p['s5_w_glu'].astype(BF16), p['s5_b_glu'].astype(F32), 512)

    y_b = lru_branch(proj, 1, 2, p['lru_conv_w'], p['lru_conv_b'], p['lru_w_a'], p['lru_b_a'],
                     p['lru_w_x'], p['lru_b_x'], p['lru_lambda'], 256)

    y_c = gdn_branch_pallas(proj, 1, 6, small0 // 128, p['gdn_conv_w'], p['gdn_a_log'], p['gdn_dt_bias'],
                            p['gdn_norm_g'], 256)

    merged = merge_branches(y_a, y_b, y_c, p['w_branch'].astype(BF16), proj, gate_col0, 1024, 512)
    x, xb = mm_res_ln(merged, p['w_out'].astype(BF16), x, p['ln_mix_g'], p['ln_mix_b'], 512, "out_proj_ln")

    q = mm(xb, p['xa_wq'].astype(BF16), BF16, 1024, 512, "xa_q")
    k = mm(memb, p['xa_wk'].astype(BF16), BF16, memb.shape[0], 512, "xa_k")
    v = mm(memb, p['xa_wv'].astype(BF16), BF16, memb.shape[0], 512, "xa_v")
    o = cross_attention(q, k, v, 512)
    x, xb = mm_res_ln(o, p['xa_wo'].astype(BF16), x, p['ln_xa_g'], p['ln_xa_b'], 512, "xa_out_ln")

    keys = p['peer_keys']
    keys = keys.reshape(PEER_HEADS * 2, PEER_NKEYS, keys.shape[-1]).astype(BF16)
    scores_t = peer_scores(xb, p['peer_wq'].astype(BF16), keys, 512)
    e1, n, rk, e2 = peer_route(scores_t, 128)
    ht = peer_dense(xb, p['peer_u'].astype(BF16), p['peer_v'].T.astype(BF16), e1, n, rk, e2, 512, 1024)
    return res_ln_t(ht, x, p['ln_ffn_g'], p['ln_ffn_b'], 512, "ffn_ln")


def kernel(x, mem, ln_mix_g, ln_mix_b, w_in, s5_a_re, s5_a_im, s5_log_dt, s5_b_re, s5_b_im, s5_c_re, s5_c_im, s5_d, s5_w_glu, s5_b_glu, lru_conv_w, lru_conv_b, lru_w_a, lru_b_a, lru_w_x, lru_b_x, lru_lambda, gdn_conv_w, gdn_a_log, gdn_dt_bias, gdn_norm_g, w_branch, w_out, ln_xa_g, ln_xa_b, xa_wq, xa_wk, xa_wv, xa_wo, ln_ffn_g, ln_ffn_b, peer_wq, peer_keys, peer_u, peer_v):
    params = dict(ln_mix_g=ln_mix_g, ln_mix_b=ln_mix_b, w_in=w_in, s5_a_re=s5_a_re, s5_a_im=s5_a_im,
                  s5_log_dt=s5_log_dt, s5_b_re=s5_b_re, s5_b_im=s5_b_im, s5_c_re=s5_c_re, s5_c_im=s5_c_im,
                  s5_d=s5_d, s5_w_glu=s5_w_glu, s5_b_glu=s5_b_glu, lru_conv_w=lru_conv_w, lru_conv_b=lru_conv_b,
                  lru_w_a=lru_w_a, lru_b_a=lru_b_a, lru_w_x=lru_w_x, lru_b_x=lru_b_x, lru_lambda=lru_lambda,
                  gdn_conv_w=gdn_conv_w, gdn_a_log=gdn_a_log, gdn_dt_bias=gdn_dt_bias, gdn_norm_g=gdn_norm_g,
                  w_branch=w_branch, w_out=w_out, ln_xa_g=ln_xa_g, ln_xa_b=ln_xa_b, xa_wq=xa_wq, xa_wk=xa_wk,
                  xa_wv=xa_wv, xa_wo=xa_wo, ln_ffn_g=ln_ffn_g, ln_ffn_b=ln_ffn_b, peer_wq=peer_wq,
                  peer_keys=peer_keys, peer_u=peer_u, peer_v=peer_v)
    bsz, seq, d = x.shape
    outs = []
    for b in range(bsz):
        xf = x[b].astype(F32)
        xb = xf.astype(BF16)
        memb = mem[b].astype(BF16)
        for l in range(DEPTH):
            xf, xb = _layer(xf, xb, memb, {k: v[l] for k, v in params.items()})
        outs.append(xf)
    return jnp.stack(outs, axis=0)
```

```python
import functools
import math

import jax
import jax.numpy as jnp
from jax import lax
from jax.experimental import pallas as pl
from jax.experimental.pallas import tpu as pltpu

F32 = jnp.float32
BF16 = jnp.bfloat16

S5_GROUP = 16
S5_STATE = 64
S5_CHUNK = 16
LRU_BLOCKS = 8
LRU_C = 8.0
CONV_K = 4
GDN_HEADS = 8
GDN_DK = 128
GDN_DV = 128
GDN_CHUNK = 64
N_BRANCH = 3
XA_HEADS = 4
PEER_HEADS = 8
PEER_NKEYS = 128
PEER_TOPK = 16
IN_PROJ_TN = 512
PEER_SUB = 512
DEPTH = 2
DN_ALPHA = (2 * DEPTH) ** 0.25
LN_EPS = 1e-5
NEG_BIG = -1e30
POS_BIG = 1e30
VMEM_LIMIT = 56 * 1024 * 1024


def _cparams(*sem):
    return pltpu.CompilerParams(dimension_semantics=sem, vmem_limit_bytes=VMEM_LIMIT)


def _gelu(x):
    return 0.5 * x * (1.0 + jnp.tanh(math.sqrt(2.0 / math.pi) * (x + 0.044715 * (x * x * x))))


def _sigmoid(x):
    return 1.0 / (1.0 + jnp.exp(-x))


def _softplus(x):
    return jnp.maximum(x, 0.0) + jnp.log1p(jnp.exp(-jnp.abs(x)))


def _layer_norm(y, g, b):
    mu = jnp.mean(y, axis=-1, keepdims=True)
    d = y - mu
    var = jnp.mean(d * d, axis=-1, keepdims=True)
    return d * lax.rsqrt(var + LN_EPS) * g + b


def _mm_kernel(a_ref, b_ref, o_ref):
    o_ref[...] = jnp.dot(a_ref[...], b_ref[...], preferred_element_type=F32).astype(o_ref.dtype)


def mm(a, b, out_dtype, tm, tn, name):
    m, k = a.shape
    n = b.shape[1]
    return pl.pallas_call(
        _mm_kernel,
        grid=(m // tm, n // tn),
        in_specs=[pl.BlockSpec((tm, k), lambda i, j: (i, 0)),
                  pl.BlockSpec((k, tn), lambda i, j: (0, j))],
        out_specs=pl.BlockSpec((tm, tn), lambda i, j: (i, j)),
        out_shape=jax.ShapeDtypeStruct((m, n), out_dtype),
        compiler_params=_cparams("parallel", "parallel"),
        name=name,
    )(a, b)


def _mm_res_ln_kernel(a_ref, w_ref, x_ref, g_ref, b_ref, o_ref, ob_ref):
    h = jnp.dot(a_ref[...], w_ref[...], preferred_element_type=F32)
    y = _layer_norm(DN_ALPHA * x_ref[...] + h, g_ref[...], b_ref[...])
    o_ref[...] = y
    ob_ref[...] = y.astype(BF16)


def mm_res_ln(a, w, x, g, b, tm, name):
    m, k = a.shape
    n = w.shape[1]
    row = lambda i: (i, 0)
    fixed = lambda i: (0, 0)
    return pl.pallas_call(
        _mm_res_ln_kernel,
        grid=(m // tm,),
        in_specs=[pl.BlockSpec((tm, k), row), pl.BlockSpec((k, n), fixed), pl.BlockSpec((tm, n), row),
                  pl.BlockSpec((1, n), fixed), pl.BlockSpec((1, n), fixed)],
        out_specs=[pl.BlockSpec((tm, n), row), pl.BlockSpec((tm, n), row)],
        out_shape=[jax.ShapeDtypeStruct((m, n), F32), jax.ShapeDtypeStruct((m, n), BF16)],
        compiler_params=_cparams("parallel"),
        name=name,
    )(a, w, x, g.reshape(1, n), b.reshape(1, n))


def _res_ln_kernel(h_ref, x_ref, g_ref, b_ref, o_ref, ob_ref):
    y = _layer_norm(DN_ALPHA * x_ref[...] + h_ref[...], g_ref[...], b_ref[...])
    o_ref[...] = y
    ob_ref[...] = y.astype(BF16)


def res_ln(h, x, g, b, tm, name):
    m, n = x.shape
    row = lambda i: (i, 0)
    fixed = lambda i: (0, 0)
    return pl.pallas_call(
        _res_ln_kernel,
        grid=(m // tm,),
        in_specs=[pl.BlockSpec((tm, n), row), pl.BlockSpec((tm, n), row),
                  pl.BlockSpec((1, n), fixed), pl.BlockSpec((1, n), fixed)],
        out_specs=[pl.BlockSpec((tm, n), row), pl.BlockSpec((tm, n), row)],
        out_shape=[jax.ShapeDtypeStruct((m, n), F32), jax.ShapeDtypeStruct((m, n), BF16)],
        compiler_params=_cparams("parallel"),
        name=name,
    )(h, x, g.reshape(1, n), b.reshape(1, n))


def s5_prepare(a_re, a_im, log_dt, b_re, b_im, c_re, c_im, d_skip):
    hi = lax.Precision.HIGHEST
    L = S5_CHUNK
    lam_re, lam_im = a_re.astype(F32), a_im.astype(F32)
    dt = jnp.exp(log_dt.astype(F32))[:, None]
    mag = jnp.exp(lam_re * dt)
    abar_re, abar_im = mag * jnp.cos(lam_im * dt), mag * jnp.sin(lam_im * dt)
    den = lam_re * lam_re + lam_im * lam_im
    num_re = abar_re - 1.0
    f_re = (num_re * lam_re + abar_im * lam_im) / den
    f_im = (abar_im * lam_re - num_re * lam_im) / den
    bre, bim = b_re.astype(F32), b_im.astype(F32)
    bbar_re = f_re[..., None] * bre - f_im[..., None] * bim
    bbar_im = f_re[..., None] * bim + f_im[..., None] * bre
    steps = jnp.arange(L + 1, dtype=F32)[:, None, None]
    pw_mag = jnp.exp(lam_re * dt * steps)
    pw_re = pw_mag * jnp.cos(lam_im * dt * steps)
    pw_im = pw_mag * jnp.sin(lam_im * dt * steps)
    cre, cim = c_re.astype(F32), c_im.astype(F32)
    ca_re = cre[None] * pw_re[:, :, None, :] - cim[None] * pw_im[:, :, None, :]
    ca_im = cre[None] * pw_im[:, :, None, :] + cim[None] * pw_re[:, :, None, :]
    kern = (jnp.einsum('dgpn,gnq->dgpq', ca_re[:L], bbar_re, precision=hi)
            - jnp.einsum('dgpn,gnq->dgpq', ca_im[:L], bbar_im, precision=hi))
    kern = kern.at[0].add(d_skip.astype(F32)[:, :, None] * jnp.eye(S5_GROUP, dtype=F32)[None])
    lag = jnp.arange(L)[None, :] - jnp.arange(L)[:, None]
    t5 = jnp.where((lag >= 0)[:, :, None, None, None], kern[jnp.clip(lag, 0, L - 1)], 0.0)
    g = a_re.shape[0]
    t_mat = t5.transpose(2, 0, 4, 1, 3).reshape(g, L * S5_GROUP, L * S5_GROUP)
    rev_re, rev_im = pw_re[L - 1::-1][:L], pw_im[L - 1::-1][:L]
    bc_re = rev_re[..., None] * bbar_re[None] - rev_im[..., None] * bbar_im[None]
    bc_im = rev_re[..., None] * bbar_im[None] + rev_im[..., None] * bbar_re[None]
    bc_mat = jnp.concatenate([bc_re.transpose(1, 0, 3, 2), bc_im.transpose(1, 0, 3, 2)],
                             axis=-1).reshape(g, L * S5_GROUP, 2 * S5_STATE)
    cc_re = ca_re[1:].transpose(1, 3, 0, 2)
    cc_im = -ca_im[1:].transpose(1, 3, 0, 2)
    cc_mat = jnp.concatenate([cc_re, cc_im], axis=1).reshape(g, 2 * S5_STATE, L * S5_GROUP)
    a1 = jnp.concatenate([pw_re[L], pw_re[L]], axis=-1)
    a2 = jnp.concatenate([-pw_im[L], pw_im[L]], axis=-1)
    return t_mat.astype(BF16), bc_mat.astype(BF16), cc_mat.astype(BF16), a1, a2


def _s5_z_kernel(u_ref, bc_ref, z_ref):
    z_ref[...] = jnp.dot(u_ref[0], bc_ref[0], preferred_element_type=F32)


def _s5_scan_kernel(z_ref, a1_ref, a2_ref, s_ref, carry_ref):
    @pl.when(pl.program_id(0) == 0)
    def _():
        carry_ref[...] = jnp.zeros_like(carry_ref)

    a1 = a1_ref[...]
    a2 = a2_ref[...]
    width = z_ref.shape[1]
    lane = lax.broadcasted_iota(jnp.int32, (1, width), 1)
    real_half = (lane & (2 * S5_STATE - 1)) < S5_STATE

    def body(c, state):
        s_ref[pl.ds(c, 1), :] = state
        swapped = jnp.where(real_half, pltpu.roll(state, width - S5_STATE, axis=1),
                            pltpu.roll(state, S5_STATE, axis=1))
        return a1 * state + a2 * swapped + z_ref[pl.ds(c, 1), :]

    carry_ref[...] = lax.fori_loop(0, z_ref.shape[0], body, carry_ref[...])


def _s5_out_kernel(u_ref, t_ref, s_ref, cc_ref, y_ref):
    y = jnp.dot(u_ref[0], t_ref[0], preferred_element_type=F32)
    y = y + jnp.dot(s_ref[...].astype(BF16), cc_ref[0], preferred_element_type=F32)
    y_ref[0] = _gelu(y)


def s5_branch(u_r, t_mat, bc_mat, cc_mat, a1, a2):
    g, c, w = u_r.shape
    ns = 2 * S5_STATE
    z = pl.pallas_call(
        _s5_z_kernel,
        grid=(g,),
        in_specs=[pl.BlockSpec((1, c, w), lambda i: (i, 0, 0)), pl.BlockSpec((1, w, ns), lambda i: (i, 0, 0))],
        out_specs=pl.BlockSpec((c, ns), lambda i: (0, i)),
        out_shape=jax.ShapeDtypeStruct((c, g * ns), F32),
        compiler_params=_cparams("parallel"),
        name="s5_chunk_inputs",
    )(u_r, bc_mat)
    cb = min(64, c)
    s_prev = pl.pallas_call(
        _s5_scan_kernel,
        grid=(c // cb,),
        in_specs=[pl.BlockSpec((cb, g * ns), lambda i: (i, 0)),
                  pl.BlockSpec((1, g * ns), lambda i: (0, 0)), pl.BlockSpec((1, g * ns), lambda i: (0, 0))],
        out_specs=pl.BlockSpec((cb, g * ns), lambda i: (i, 0)),
        out_shape=jax.ShapeDtypeStruct((c, g * ns), F32),
        scratch_shapes=[pltpu.VMEM((1, g * ns), F32)],
        compiler_params=_cparams("arbitrary"),
        name="s5_chunk_scan",
    )(z, a1.reshape(1, g * ns), a2.reshape(1, g * ns))
    return pl.pallas_call(
        _s5_out_kernel,
        grid=(g,),
        in_specs=[pl.BlockSpec((1, c, w), lambda i: (i, 0, 0)), pl.BlockSpec((1, w, w), lambda i: (i, 0, 0)),
                  pl.BlockSpec((c, ns), lambda i: (0, i)), pl.BlockSpec((1, ns, w), lambda i: (i, 0, 0))],
        out_specs=pl.BlockSpec((1, c, w), lambda i: (i, 0, 0)),
        out_shape=jax.ShapeDtypeStruct((g, c, w), F32),
        compiler_params=_cparams("parallel"),
        name="s5_outputs",
    )(u_r, t_mat, s_prev, cc_mat)


def _glu_kernel(y_ref, w_ref, b_ref, o_ref):
    y = y_ref[...]
    gate = jnp.dot(y.astype(BF16), w_ref[...], preferred_element_type=F32) + b_ref[...]
    o_ref[...] = (y * _sigmoid(gate)).astype(o_ref.dtype)


def glu(y, w, b, tm):
    m, n = y.shape
    return pl.pallas_call(
        _glu_kernel,
        grid=(m // tm,),
        in_specs=[pl.BlockSpec((tm, n), lambda i: (i, 0)), pl.BlockSpec((n, n), lambda i: (0, 0)),
                  pl.BlockSpec((1, n), lambda i: (0, 0))],
        out_specs=pl.BlockSpec((tm, n), lambda i: (i, 0)),
        out_shape=jax.ShapeDtypeStruct((m, n), BF16),
        compiler_params=_cparams("parallel"),
        name="s5_glu",
    )(y, w, b.reshape(1, n))


def _lru_kernel(x_ref, gate_ref, cw_ref, cb_ref, wa_ref, ba_ref, wx_ref, bx_ref, lam_ref, o_ref,
                tail_ref, h_ref, a_ref, b_ref):
    tb, width = x_ref.shape
    blk = width // LRU_BLOCKS

    @pl.when(pl.program_id(0) == 0)
    def _():
        tail_ref[...] = jnp.zeros_like(tail_ref)
        h_ref[...] = jnp.zeros_like(h_ref)

    x = x_ref[...]
    ext = jnp.concatenate([tail_ref[...], x], axis=0)
    tail_ref[...] = x[tb - 8:, :]
    xc = cb_ref[...] + sum(cw_ref[j:j + 1, :] * ext[8 - (CONV_K - 1) + j: 8 - (CONV_K - 1) + j + tb, :]
                           for j in range(CONV_K))
    xcb = xc.astype(BF16)
    r = jnp.concatenate([jnp.dot(xcb[:, i * blk:(i + 1) * blk], wa_ref[i], preferred_element_type=F32)
                         for i in range(LRU_BLOCKS)], axis=1)
    gi = jnp.concatenate([jnp.dot(xcb[:, i * blk:(i + 1) * blk], wx_ref[i], preferred_element_type=F32)
                          for i in range(LRU_BLOCKS)], axis=1)
    r = _sigmoid(r + ba_ref[...])
    gi = _sigmoid(gi + bx_ref[...])
    log_a = -LRU_C * r * _softplus(-lam_ref[...])
    a = jnp.exp(log_a)
    a_ref[...] = a
    b_ref[...] = jnp.sqrt(-jnp.tanh(log_a) * (a * a + 1.0)) * (gi * xc)

    def body(t, h):
        h = a_ref[pl.ds(t, 1), :] * h + b_ref[pl.ds(t, 1), :]
        b_ref[pl.ds(t, 1), :] = h
        return h

    h_ref[...] = lax.fori_loop(0, tb, body, h_ref[...], unroll=8)
    o_ref[...] = (b_ref[...] * _gelu(gate_ref[...])).astype(o_ref.dtype)


def lru_branch(proj, x_col, gate_col, conv_w, conv_b, w_a, b_a, w_x, b_x, lam, tb):
    s = proj.shape[0]
    width = conv_w.shape[1]
    fixed2 = lambda i: (0, 0)
    fixed3 = lambda i: (0, 0, 0)
    vec = lambda v: v.reshape(1, width).astype(F32)
    return pl.pallas_call(
        _lru_kernel,
        grid=(s // tb,),
        in_specs=[pl.BlockSpec((tb, width), lambda i: (i, x_col)), pl.BlockSpec((tb, width), lambda i: (i, gate_col)),
                  pl.BlockSpec((CONV_K, width), fixed2), pl.BlockSpec((1, width), fixed2),
                  pl.BlockSpec(w_a.shape, fixed3), pl.BlockSpec((1, width), fixed2),
                  pl.BlockSpec(w_x.shape, fixed3), pl.BlockSpec((1, width), fixed2),
                  pl.BlockSpec((1, width), fixed2)],
        out_specs=pl.BlockSpec((tb, width), lambda i: (i, 0)),
        out_shape=jax.ShapeDtypeStruct((s, width), BF16),
        scratch_shapes=[pltpu.VMEM((8, width), F32), pltpu.VMEM((1, width), F32),
                        pltpu.VMEM((tb, width), F32), pltpu.VMEM((tb, width), F32)],
        compiler_params=_cparams("arbitrary"),
        name="rglru",
    )(proj, proj, conv_w.astype(F32), vec(conv_b), w_a.astype(BF16), vec(b_a), w_x.astype(BF16), vec(b_x), vec(lam))


def _split3(x):
    hi = x.astype(BF16)
    r1 = x - hi.astype(F32)
    mid = r1.astype(BF16)
    lo = (r1 - mid.astype(F32)).astype(BF16)
    return hi, mid, lo


def _nt(a, b):
    return lax.dot_general(a, b, (((1,), (1,)), ((), ())), preferred_element_type=F32)


def _gdn_kernel(qkv_ref, og_ref, ba_ref, cw_ref, alog_ref, dtb_ref, ng_ref, o_ref,
                tail_ref, state_ref, q_s, k_s, v_s, beta_s, g_s):
    tb = qkv_ref.shape[0]
    c = GDN_CHUNK
    dk, dv = GDN_DK, GDN_DV
    nqk = GDN_HEADS * dk

    @pl.when(pl.program_id(0) == 0)
    def _():
        tail_ref[...] = jnp.zeros_like(tail_ref)
        state_ref[...] = jnp.zeros_like(state_ref)

    x = qkv_ref[...]
    ext = jnp.concatenate([tail_ref[...], x], axis=0)
    tail_ref[...] = x[tb - 8:, :]
    xc = sum(cw_ref[j:j + 1, :] * ext[8 - (CONV_K - 1) + j: 8 - (CONV_K - 1) + j + tb, :] for j in range(CONV_K))
    xc = xc * _sigmoid(xc)
    for h in range(GDN_HEADS):
        qh = xc[:, h * dk:(h + 1) * dk]
        kh = xc[:, nqk + h * dk: nqk + (h + 1) * dk]
        q_s[:, h * dk:(h + 1) * dk] = qh * lax.rsqrt(jnp.sum(qh * qh, axis=-1, keepdims=True) + 1e-6) * dk ** -0.5
        k_s[:, h * dk:(h + 1) * dk] = kh * lax.rsqrt(jnp.sum(kh * kh, axis=-1, keepdims=True) + 1e-6)
    v_s[...] = xc[:, 2 * nqk:]
    ba = ba_ref[...]
    beta_s[...] = _sigmoid(ba)
    g_s[...] = -jnp.exp(alog_ref[...]) * _softplus(ba + dtb_ref[...])

    ri = lax.broadcasted_iota(jnp.int32, (c, c), 0)
    ci = lax.broadcasted_iota(jnp.int32, (c, c), 1)
    causal = ri >= ci
    strict = ri > ci
    tril = jnp.where(causal, 1.0, 0.0).astype(BF16)
    eye = jnp.where(ri == ci, 1.0, 0.0)
    lane = lax.broadcasted_iota(jnp.int32, (c, 128), 1)
    pad_x = jnp.where((lane >= 3) & (lane < 6), 1.0, 0.0)
    pad_y = jnp.where(lane < 3, 1.0, 0.0)

    def chunk(ic, _):
        r0 = pl.multiple_of(ic * c, c)
        rows = pl.ds(r0, c)
        g_parts = _split3(g_s[rows, :])
        gcum = sum(jnp.dot(tril, part, preferred_element_type=F32) for part in g_parts)
        beta = beta_s[rows, :]
        heads = range(GDN_HEADS)
        hs = [slice(h * dk, (h + 1) * dk) for h in heads]
        dot = functools.partial(jnp.dot, preferred_element_type=F32)
        gcol = [gcum[:, GDN_HEADS + h:GDN_HEADS + h + 1] for h in heads]
        q = [q_s[rows, hs[h]] for h in heads]
        k = [k_s[rows, hs[h]] for h in heads]
        kbf = [k[h].astype(BF16) for h in heads]
        kb = [k[h] * beta[:, h:h + 1] for h in heads]
        vb = [(v_s[rows, hs[h]] * beta[:, h:h + 1]).astype(BF16) for h in heads]
        decay = []
        for h in heads:
            hi, mid, lo = (p.astype(F32) for p in _split3(gcol[h]))
            xa = jnp.where(lane == 0, hi, jnp.where(lane == 1, mid, jnp.where(lane == 2, lo, pad_x)))
            ya = jnp.where(lane == 3, -hi, jnp.where(lane == 4, -mid, jnp.where(lane == 5, -lo, pad_y)))
            decay.append(jnp.exp(jnp.where(causal, _nt(xa.astype(BF16), ya.astype(BF16)), -jnp.inf)))
        neg_a = [jnp.where(strict, -_nt(kb[h].astype(BF16), kbf[h]) * decay[h], 0.0) for h in heads]
        intra = [jnp.where(causal, _nt(q[h].astype(BF16), kbf[h]) * decay[h], 0.0).astype(BF16) for h in heads]
        t_off = list(neg_a)
        pw = list(neg_a)
        for _ in range(5):
            pwb = [pw[h].astype(BF16) for h in heads]
            pw = [dot(pwb[h], pwb[h]) for h in heads]
            t_off = [t_off[h] + pw[h] + dot(t_off[h].astype(BF16), pw[h].astype(BF16)) for h in heads]
        t_inv = [(eye + t_off[h]).astype(BF16) for h in heads]
        eg = [jnp.exp(gcol[h]) for h in heads]
        w = [dot(t_inv[h], (kb[h] * eg[h]).astype(BF16)).astype(BF16) for h in heads]
        u = [dot(t_inv[h], vb[h]) for h in heads]
        state = [state_ref[h] for h in heads]
        sb = [state[h].astype(BF16) for h in heads]
        vnb = [(u[h] - dot(w[h], sb[h])).astype(BF16) for h in heads]
        g_last = [gcol[h][c - 1:c, :] for h in heads]
        k_dec = [(k[h] * jnp.exp(g_last[h] - gcol[h])).astype(BF16) for h in heads]
        for h in heads:
            state_ref[h] = state[h] * jnp.exp(g_last[h]) + lax.dot_general(
                k_dec[h], vnb[h], (((0,), (0,)), ((), ())), preferred_element_type=F32)
        o = [dot((q[h] * eg[h]).astype(BF16), sb[h]) + dot(intra[h], vnb[h]) for h in heads]
        for h in heads:
            on = o[h] * lax.rsqrt(jnp.mean(o[h] * o[h], axis=-1, keepdims=True) + 1e-6) * ng_ref[...]
            og = og_ref[rows, hs[h]]
            o_ref[rows, hs[h]] = (on * (og * _sigmoid(og))).astype(o_ref.dtype)
        return 0

    lax.fori_loop(0, tb // c, chunk, 0)


def gdn_branch_pallas(proj, qkv_col, og_col, ba_col, conv_w, a_log, dt_bias, norm_g, tb):
    s = proj.shape[0]
    nv = GDN_HEADS * GDN_DV
    nqkv = conv_w.shape[1]
    pad = jnp.zeros((GDN_HEADS,), F32)
    lanes = lambda v: jnp.concatenate([pad, v.astype(F32), jnp.zeros((128 - 2 * GDN_HEADS,), F32)]).reshape(1, 128)
    fixed = lambda i: (0, 0)
    return pl.pallas_call(
        _gdn_kernel,
        grid=(s // tb,),
        in_specs=[pl.BlockSpec((tb, nqkv), lambda i: (i, qkv_col)), pl.BlockSpec((tb, nv), lambda i: (i, og_col)),
                  pl.BlockSpec((tb, 128), lambda i: (i, ba_col)), pl.BlockSpec((CONV_K, nqkv), fixed),
                  pl.BlockSpec((1, 128), fixed), pl.BlockSpec((1, 128), fixed), pl.BlockSpec((1, GDN_DV), fixed)],
        out_specs=pl.BlockSpec((tb, nv), lambda i: (i, 0)),
        out_shape=jax.ShapeDtypeStruct((s, nv), BF16),
        scratch_shapes=[pltpu.VMEM((8, nqkv), F32), pltpu.VMEM((GDN_HEADS, GDN_DK, GDN_DV), F32),
                        pltpu.VMEM((tb, nv), F32), pltpu.VMEM((tb, nv), F32), pltpu.VMEM((tb, nv), F32),
                        pltpu.VMEM((tb, 128), F32), pltpu.VMEM((tb, 128), F32)],
        compiler_params=_cparams("arbitrary"),
        name="gated_deltanet",
    )(proj, proj, proj, conv_w.astype(F32), lanes(a_log), lanes(dt_bias), norm_g.astype(F32).reshape(1, GDN_DV))


def _merge_kernel(ya_ref, yb_ref, yc_ref, wb_ref, ga_ref, gb_ref, gc_ref, o_ref):
    acc = None
    for n, (y_ref, g_ref) in enumerate(((ya_ref, ga_ref), (yb_ref, gb_ref), (yc_ref, gc_ref))):
        d = jnp.dot(y_ref[...], wb_ref[n], preferred_element_type=F32) * _sigmoid(g_ref[...])
        acc = d if acc is None else acc + d
    o_ref[...] = acc.astype(o_ref.dtype)


def merge_branches(ya, yb, yc, w_branch, proj, gate_col0, tm, tn):
    s, width = ya.shape
    d = w_branch.shape[2]
    nj = d // tn
    c0 = gate_col0 // tn
    yspec = pl.BlockSpec((tm, width), lambda i, j: (i, 0))
    gspec = lambda n: pl.BlockSpec((tm, tn), lambda i, j: (i, c0 + n * nj + j))
    return pl.pallas_call(
        _merge_kernel,
        grid=(s // tm, nj),
        in_specs=[yspec, yspec, yspec, pl.BlockSpec((N_BRANCH, width, tn), lambda i, j: (0, 0, j)),
                  gspec(0), gspec(1), gspec(2)],
        out_specs=pl.BlockSpec((tm, tn), lambda i, j: (i, j)),
        out_shape=jax.ShapeDtypeStruct((s, d), BF16),
        compiler_params=_cparams("parallel", "parallel"),
        name="merge_branches",
    )(ya, yb, yc, w_branch, proj, proj, proj)


def _xattn_kernel(q_ref, k_ref, v_ref, o_ref):
    d = q_ref.shape[1]
    hd = d // XA_HEADS
    outs = []
    for h in range(XA_HEADS):
        sl = slice(h * hd, (h + 1) * hd)
        s = lax.dot_general(q_ref[:, sl], k_ref[:, sl], (((1,), (1,)), ((), ())),
                            preferred_element_type=F32) * hd ** -0.5
        s = s - jnp.max(s, axis=-1, keepdims=True)
        p = jnp.exp(s)
        p = p / jnp.sum(p, axis=-1, keepdims=True)
        outs.append(jnp.dot(p.astype(BF16), v_ref[:, sl], preferred_element_type=F32))
    o_ref[...] = jnp.concatenate(outs, axis=1).astype(o_ref.dtype)


def cross_attention(q, k, v, tq):
    s, d = q.shape
    m = k.shape[0]
    return pl.pallas_call(
        _xattn_kernel,
        grid=(s // tq,),
        in_specs=[pl.BlockSpec((tq, d), lambda i: (i, 0)), pl.BlockSpec((m, d), lambda i: (0, 0)),
                  pl.BlockSpec((m, d), lambda i: (0, 0))],
        out_specs=pl.BlockSpec((tq, d), lambda i: (i, 0)),
        out_shape=jax.ShapeDtypeStruct((s, d), BF16),
        compiler_params=_cparams("parallel"),
        name="cross_attention",
    )(q, k, v)


def _peer_scores_kernel(x_ref, wq_ref, keys_ref, o_ref):
    q = jnp.dot(x_ref[...], wq_ref[...], preferred_element_type=F32).astype(BF16)
    nblk, nk, dk = keys_ref.shape
    for b in range(nblk):
        o_ref[b * nk:(b + 1) * nk, :] = lax.dot_general(
            keys_ref[b], q[:, b * dk:(b + 1) * dk], (((1,), (1,)), ((), ())), preferred_element_type=F32)


def peer_scores(xb, wq, keys, tm):
    s, d = xb.shape
    nblk, nk, dk = keys.shape
    return pl.pallas_call(
        _peer_scores_kernel,
        grid=(s // tm,),
        in_specs=[pl.BlockSpec((tm, d), lambda i: (i, 0)), pl.BlockSpec(wq.shape, lambda i: (0, 0)),
                  pl.BlockSpec(keys.shape, lambda i: (0, 0, 0))],
        out_specs=pl.BlockSpec((nblk * nk, tm), lambda i: (0, i)),
        out_shape=jax.ShapeDtypeStruct((nblk * nk, s), F32),
        compiler_params=_cparams("parallel"),
        name="peer_scores",
    )(xb, wq, keys)


def _select_round(work, rows, sentinel):
    m = jnp.max(work, axis=0, keepdims=True)
    idx = jnp.min(jnp.where(work == m, rows, sentinel), axis=0, keepdims=True)
    hit = rows == idx
    return m, idx, hit, jnp.where(hit, -jnp.inf, work)


def _peer_route_kernel(s_ref, sel0_ref, sel1_ref, pad_ref, grp_ref, e1_ref, n_ref, rk_ref, e2_ref):
    nk, k = PEER_NKEYS, PEER_TOPK
    tt = s_ref.shape[1]
    nc = sel0_ref.shape[0]
    rows = lax.broadcasted_iota(jnp.int32, (nk, tt), 0)
    rows_k = lax.broadcasted_iota(jnp.int32, (k, tt), 0)
    rows_c = lax.broadcasted_iota(jnp.int32, (nc, tt), 0)

    def head(h, _):
        s0 = s_ref[pl.ds(pl.multiple_of(h * 2 * nk, nk), nk), :]
        s1 = s_ref[pl.ds(pl.multiple_of(h * 2 * nk + nk, nk), nk), :]

        def stage1(r, carry):
            w0, ts0, ti0, w1, ts1, rk1 = carry
            m0, i0, _, w0 = _select_round(w0, rows, nk)
            m1, _, hit1, w1 = _select_round(w1, rows, nk)
            cur = rows_k == r
            return (w0, jnp.where(cur, m0, ts0), jnp.where(cur, i0, ti0),
                    w1, jnp.where(cur, m1, ts1), jnp.where(hit1, r, rk1))

        zk = jnp.zeros((k, tt), F32)
        w0, ts0, ti0, w1, ts1, rk1 = lax.fori_loop(
            0, k, stage1, (s0, zk, jnp.zeros((k, tt), jnp.int32), s1, zk, jnp.full((nk, tt), k, jnp.int32)))

        def pick(sel_ref, vals):
            return sum(jnp.dot(sel_ref[...], part, preferred_element_type=F32) for part in _split3(vals))

        cand = pick(sel0_ref, ts0) + pick(sel1_ref, ts1) + pad_ref[...]

        def stage2(r, carry):
            wc, best = carry
            m, _, _, wc = _select_round(wc, rows_c, nc)
            return wc, jnp.where(rows_k == r, m, best)

        wc, best = lax.fori_loop(0, k, stage2, (cand, zk))
        taken = jnp.where(wc == -jnp.inf, 1.0, 0.0).astype(BF16)
        n_rank = jnp.dot(grp_ref[...], taken, preferred_element_type=F32)
        n_key = jnp.zeros((nk, tt), F32)
        for i in range(k):
            n_key = jnp.where(rows == ti0[i:i + 1, :], n_rank[i:i + 1, :], n_key)
        norm = jnp.sum(jnp.exp(best - best[0:1, :]), axis=0, keepdims=True)
        out = pl.ds(pl.multiple_of(h * nk, nk), nk)
        e1_ref[out, :] = jnp.where(w0 == -jnp.inf, jnp.exp(s0 - ts0[0:1, :]), 0.0)
        n_ref[out, :] = n_key
        rk_ref[out, :] = rk1.astype(F32).astype(BF16)
        e2_ref[out, :] = (jnp.where(rk1 < k, jnp.exp(s1 - ts1[0:1, :]), 0.0) / norm).astype(BF16)
        return 0

    lax.fori_loop(0, PEER_HEADS, head, 0)


def peer_route(scores_t, tt):
    r, s = scores_t.shape
    hk = r // 2
    k = PEER_TOPK
    pairs = [(i, j) for i in range(k) for j in range(k // (i + 1))]
    nc = -(-len(pairs) // 8) * 8
    ranks = jnp.arange(k)[None, :]
    first = jnp.array([p[0] for p in pairs] + [-1] * (nc - len(pairs)))[:, None]
    second = jnp.array([p[1] for p in pairs] + [-1] * (nc - len(pairs)))[:, None]
    sel0 = (first == ranks).astype(BF16)
    sel1 = (second == ranks).astype(BF16)
    pad = jnp.where(first >= 0, 0.0, -jnp.inf).astype(F32)
    fixed = lambda i: (0, 0)
    out = pl.BlockSpec((hk, tt), lambda i: (0, i))
    return pl.pallas_call(
        _peer_route_kernel,
        grid=(s // tt,),
        in_specs=[pl.BlockSpec((r, tt), lambda i: (0, i)), pl.BlockSpec((nc, k), fixed), pl.BlockSpec((nc, k), fixed),
                  pl.BlockSpec((nc, 1), fixed), pl.BlockSpec((k, nc), fixed)],
        out_specs=[out, out, out, out],
        out_shape=[jax.ShapeDtypeStruct((hk, s), F32), jax.ShapeDtypeStruct((hk, s), F32),
                   jax.ShapeDtypeStruct((hk, s), BF16), jax.ShapeDtypeStruct((hk, s), BF16)],
        compiler_params=_cparams("parallel"),
        name="peer_route",
    )(scores_t, sel0, sel1, pad, sel0.T)


def _peer_dense_kernel(x_ref, u_ref, vt_ref, e1_ref, n_ref, rk_ref, e2_ref, o_ref, w_s, h_s):
    @pl.when(pl.program_id(1) == 0)
    def _():
        o_ref[...] = jnp.zeros_like(o_ref)

    te = u_ref.shape[0]
    tt = x_ref.shape[0]
    na = te // PEER_NKEYS
    nsub = te // PEER_SUB
    asub = na // nsub

    def gates(sb):
        for ai in range(sb * asub, (sb + 1) * asub):
            w = None
            for hd in range(PEER_HEADS):
                rows = slice(hd * PEER_NKEYS, (hd + 1) * PEER_NKEYS)
                n = jnp.broadcast_to(n_ref[hd, ai:ai + 1, :].astype(BF16), (PEER_NKEYS, tt))
                e1 = jnp.broadcast_to(e1_ref[hd, ai:ai + 1, :].astype(BF16), (PEER_NKEYS, tt))
                term = jnp.where(rk_ref[rows, :] < n, e2_ref[rows, :], jnp.zeros((), BF16)) * e1
                w = term if w is None else w + term
            w_s[ai * PEER_NKEYS:(ai + 1) * PEER_NKEYS, :] = w

    def first(sb):
        rows = slice(sb * PEER_SUB, (sb + 1) * PEER_SUB)
        h_s[rows, :] = lax.dot_general(u_ref[rows, :], x_ref[...], (((1,), (1,)), ((), ())),
                                       preferred_element_type=F32)

    def second(sb):
        rows = slice(sb * PEER_SUB, (sb + 1) * PEER_SUB)
        act = _gelu(h_s[rows, :]).astype(BF16) * w_s[rows, :]
        o_ref[...] += jnp.dot(vt_ref[:, rows], act, preferred_element_type=F32)

    first(0)
    gates(0)
    for sb in range(nsub):
        if sb + 1 < nsub:
            first(sb + 1)
            gates(sb + 1)
        second(sb)


def peer_dense(xb, u_tab, vt_tab, e1t, nt, rkt, e2t, tt, te):
    s, d = xb.shape
    ne = u_tab.shape[0]
    hk = e1t.shape[0]
    side = pl.BlockSpec((hk, tt), lambda i, j: (0, i))
    na = te // PEER_NKEYS
    first = pl.BlockSpec((PEER_HEADS, na, tt), lambda i, j: (0, j, i))
    e1t = e1t.reshape(PEER_HEADS, PEER_NKEYS, s)
    nt = nt.reshape(PEER_HEADS, PEER_NKEYS, s)
    return pl.pallas_call(
        _peer_dense_kernel,
        grid=(s // tt, ne // te),
        in_specs=[pl.BlockSpec((tt, d), lambda i, j: (i, 0)), pl.BlockSpec((te, d), lambda i, j: (j, 0)),
                  pl.BlockSpec((d, te), lambda i, j: (0, j)), first, first, side, side],
        out_specs=pl.BlockSpec((d, tt), lambda i, j: (0, i)),
        out_shape=jax.ShapeDtypeStruct((d, s), F32),
        scratch_shapes=[pltpu.VMEM((te, tt), BF16), pltpu.VMEM((te, tt), F32)],
        compiler_params=_cparams("parallel", "arbitrary"),
        name="peer_dense",
    )(xb, u_tab, vt_tab, e1t, nt, rkt, e2t)


def _res_ln_t_kernel(ht_ref, x_ref, g_ref, b_ref, o_ref, ob_ref):
    y = _layer_norm(DN_ALPHA * x_ref[...] + ht_ref[...].T, g_ref[...], b_ref[...])
    o_ref[...] = y
    ob_ref[...] = y.astype(BF16)


def res_ln_t(ht, x, g, b, tm, name):
    m, n = x.shape
    row = lambda i: (i, 0)
    fixed = lambda i: (0, 0)
    return pl.pallas_call(
        _res_ln_t_kernel,
        grid=(m // tm,),
        in_specs=[pl.BlockSpec((n, tm), lambda i: (0, i)), pl.BlockSpec((tm, n), row),
                  pl.BlockSpec((1, n), fixed), pl.BlockSpec((1, n), fixed)],
        out_specs=[pl.BlockSpec((tm, n), row), pl.BlockSpec((tm, n), row)],
        out_shape=[jax.ShapeDtypeStruct((m, n), F32), jax.ShapeDtypeStruct((m, n), BF16)],
        compiler_params=_cparams("parallel"),
        name=name,
    )(ht, x, g.reshape(1, n), b.reshape(1, n))


def _layer(x, xb, memb, p):
    s, d = x.shape
    half = d // 2
    w_in = p['w_in']
    n_main = 7 * half
    n_small = 2 * GDN_HEADS
    n_used = w_in.shape[1]
    n_proj = -(-n_used // IN_PROJ_TN) * IN_PROJ_TN
    w_cat = jnp.concatenate([w_in[:, :n_main], w_in[:, n_main + n_small:], w_in[:, n_main:n_main + n_small],
                             jnp.zeros((d, n_proj - n_used), w_in.dtype)], axis=1).astype(BF16)
    proj = mm(xb, w_cat, F32, 1024, IN_PROJ_TN, "in_proj")
    gate_col0 = n_main
    small0 = n_main + N_BRANCH * d

    g_s5 = half // S5_GROUP
    nchunk = s // S5_CHUNK
    u_r = (proj[:, :half].astype(BF16).reshape(nchunk, S5_CHUNK, g_s5, S5_GROUP)
           .transpose(2, 0, 1, 3).reshape(g_s5, nchunk, S5_CHUNK * S5_GROUP))
    prep = s5_prepare(p['s5_a_re'], p['s5_a_im'], p['s5_log_dt'], p['s5_b_re'], p['s5_b_im'],
                      p['s5_c_re'], p['s5_c_im'], p['s5_d'])
    y_r = s5_branch(u_r, *prep)
    y_s5 = (y_r.reshape(g_s5, nchunk, S5_CHUNK, S5_GROUP).transpose(1, 2, 0, 3).reshape(s, half))
    y_a = glu(y_s5, p['s5_w_glu'].astype(BF16), p['s5_b_glu'].astype(F32), 512)

    y_b = lru_branch(proj, 1, 2, p['lru_conv_w'], p['lru_conv_b'], p['lru_w_a'], p['lru_b_a'],
                     p['lru_w_x'], p['lru_b_x'], p['lru_lambda'], 256)

    y_c = gdn_branch_pallas(proj, 1, 6, small0 // 128, p['gdn_conv_w'], p['gdn_a_log'], p['gdn_dt_bias'],
                            p['gdn_norm_g'], 256)

    merged = merge_branches(y_a, y_b, y_c, p['w_branch'].astype(BF16), proj, gate_col0, 1024, 512)
    x, xb = mm_res_ln(merged, p['w_out'].astype(BF16), x, p['ln_mix_g'], p['ln_mix_b'], 512, "out_proj_ln")

    q = mm(xb, p['xa_wq'].astype(BF16), BF16, 1024, 512, "xa_q")
    k = mm(memb, p['xa_wk'].astype(BF16), BF16, memb.shape[0], 512, "xa_k")
    v = mm(memb, p['xa_wv'].astype(BF16), BF16, memb.shape[0], 512, "xa_v")
    o = cross_attention(q, k, v, 512)
    x, xb = mm_res_ln(o, p['xa_wo'].astype(BF16), x, p['ln_xa_g'], p['ln_xa_b'], 512, "xa_out_ln")

    keys = p['peer_keys']
    keys = keys.reshape(PEER_HEADS * 2, PEER_NKEYS, keys.shape[-1]).astype(BF16)
    scores_t = peer_scores(xb, p['peer_wq'].astype(BF16), keys, 512)
    e1, n, rk, e2 = peer_route(scores_t, 128)
    ht = peer_dense(xb, p['peer_u'].astype(BF16), p['peer_v'].T.astype(BF16), e1, n, rk, e2, 512, 1024)
    return res_ln_t(ht, x, p['ln_ffn_g'], p['ln_ffn_b'], 512, "ffn_ln")


def kernel(x, mem, ln_mix_g, ln_mix_b, w_in, s5_a_re, s5_a_im, s5_log_dt, s5_b_re, s5_b_im, s5_c_re, s5_c_im, s5_d, s5_w_glu, s5_b_glu, lru_conv_w, lru_conv_b, lru_w_a, lru_b_a, lru_w_x, lru_b_x, lru_lambda, gdn_conv_w, gdn_a_log, gdn_dt_bias, gdn_norm_g, w_branch, w_out, ln_xa_g, ln_xa_b, xa_wq, xa_wk, xa_wv, xa_wo, ln_ffn_g, ln_ffn_b, peer_wq, peer_keys, peer_u, peer_v):
    params = dict(ln_mix_g=ln_mix_g, ln_mix_b=ln_mix_b, w_in=w_in, s5_a_re=s5_a_re, s5_a_im=s5_a_im,
                  s5_log_dt=s5_log_dt, s5_b_re=s5_b_re, s5_b_im=s5_b_im, s5_c_re=s5_c_re, s5_c_im=s5_c_im,
                  s5_d=s5_d, s5_w_glu=s5_w_glu, s5_b_glu=s5_b_glu, lru_conv_w=lru_conv_w, lru_conv_b=lru_conv_b,
                  lru_w_a=lru_w_a, lru_b_a=lru_b_a, lru_w_x=lru_w_x, lru_b_x=lru_b_x, lru_lambda=lru_lambda,
                  gdn_conv_w=gdn_conv_w, gdn_a_log=gdn_a_log, gdn_dt_bias=gdn_dt_bias, gdn_norm_g=gdn_norm_g,
                  w_branch=w_branch, w_out=w_out, ln_xa_g=ln_xa_g, ln_xa_b=ln_xa_b, xa_wq=xa_wq, xa_wk=xa_wk,
                  xa_wv=xa_wv, xa_wo=xa_wo, ln_ffn_g=ln_ffn_g, ln_ffn_b=ln_ffn_b, peer_wq=peer_wq,
                  peer_keys=peer_keys, peer_u=peer_u, peer_v=peer_v)
    bsz, seq, d = x.shape
    outs = []
    for b in range(bsz):
        xf = x[b].astype(F32)
        xb = xf.astype(BF16)
        memb = mem[b].astype(BF16)
        for l in range(DEPTH):
            xf, xb = _layer(xf, xb, memb, {k: v[l] for k, v in params.items()})
        outs.append(xf)
    return jnp.stack(outs, axis=0)
```

```python
import functools
import math

import jax
import jax.numpy as jnp
from jax import lax
from jax.experimental import pallas as pl
from jax.experimental.pallas import tpu as pltpu

F32 = jnp.float32
BF16 = jnp.bfloat16

S5_GROUP = 16
S5_STATE = 64
S5_CHUNK = 16
LRU_BLOCKS = 8
LRU_C = 8.0
CONV_K = 4
GDN_HEADS = 8
GDN_DK = 128
GDN_DV = 128
GDN_CHUNK = 64
N_BRANCH = 3
XA_HEADS = 4
PEER_HEADS = 8
PEER_NKEYS = 128
PEER_TOPK = 16
IN_PROJ_TN = 512
PEER_SUB = 512
DEPTH = 2
DN_ALPHA = (2 * DEPTH) ** 0.25
LN_EPS = 1e-5
NEG_BIG = -1e30
POS_BIG = 1e30
VMEM_LIMIT = 56 * 1024 * 1024


def _cparams(*sem):
    return pltpu.CompilerParams(dimension_semantics=sem, vmem_limit_bytes=VMEM_LIMIT)


def _gelu(x):
    return 0.5 * x * (1.0 + jnp.tanh(math.sqrt(2.0 / math.pi) * (x + 0.044715 * (x * x * x))))


def _sigmoid(x):
    return 1.0 / (1.0 + jnp.exp(-x))


def _softplus(x):
    return jnp.maximum(x, 0.0) + jnp.log1p(jnp.exp(-jnp.abs(x)))


def _layer_norm(y, g, b):
    mu = jnp.mean(y, axis=-1, keepdims=True)
    d = y - mu
    var = jnp.mean(d * d, axis=-1, keepdims=True)
    return d * lax.rsqrt(var + LN_EPS) * g + b


def _mm_kernel(a_ref, b_ref, o_ref):
    o_ref[...] = jnp.dot(a_ref[...], b_ref[...], preferred_element_type=F32).astype(o_ref.dtype)


def mm(a, b, out_dtype, tm, tn, name):
    m, k = a.shape
    n = b.shape[1]
    return pl.pallas_call(
        _mm_kernel,
        grid=(m // tm, n // tn),
        in_specs=[pl.BlockSpec((tm, k), lambda i, j: (i, 0)),
                  pl.BlockSpec((k, tn), lambda i, j: (0, j))],
        out_specs=pl.BlockSpec((tm, tn), lambda i, j: (i, j)),
        out_shape=jax.ShapeDtypeStruct((m, n), out_dtype),
        compiler_params=_cparams("parallel", "parallel"),
        name=name,
    )(a, b)


def _mm_res_ln_kernel(a_ref, w_ref, x_ref, g_ref, b_ref, o_ref, ob_ref):
    h = jnp.dot(a_ref[...], w_ref[...], preferred_element_type=F32)
    y = _layer_norm(DN_ALPHA * x_ref[...] + h, g_ref[...], b_ref[...])
    o_ref[...] = y
    ob_ref[...] = y.astype(BF16)


def mm_res_ln(a, w, x, g, b, tm, name):
    m, k = a.shape
    n = w.shape[1]
    row = lambda i: (i, 0)
    fixed = lambda i: (0, 0)
    return pl.pallas_call(
        _mm_res_ln_kernel,
        grid=(m // tm,),
        in_specs=[pl.BlockSpec((tm, k), row), pl.BlockSpec((k, n), fixed), pl.BlockSpec((tm, n), row),
                  pl.BlockSpec((1, n), fixed), pl.BlockSpec((1, n), fixed)],
        out_specs=[pl.BlockSpec((tm, n), row), pl.BlockSpec((tm, n), row)],
        out_shape=[jax.ShapeDtypeStruct((m, n), F32), jax.ShapeDtypeStruct((m, n), BF16)],
        compiler_params=_cparams("parallel"),
        name=name,
    )(a, w, x, g.reshape(1, n), b.reshape(1, n))


def _res_ln_kernel(h_ref, x_ref, g_ref, b_ref, o_ref, ob_ref):
    y = _layer_norm(DN_ALPHA * x_ref[...] + h_ref[...], g_ref[...], b_ref[...])
    o_ref[...] = y
    ob_ref[...] = y.astype(BF16)


def res_ln(h, x, g, b, tm, name):
    m, n = x.shape
    row = lambda i: (i, 0)
    fixed = lambda i: (0, 0)
    return pl.pallas_call(
        _res_ln_kernel,
        grid=(m // tm,),
        in_specs=[pl.BlockSpec((tm, n), row), pl.BlockSpec((tm, n), row),
                  pl.BlockSpec((1, n), fixed), pl.BlockSpec((1, n), fixed)],
        out_specs=[pl.BlockSpec((tm, n), row), pl.BlockSpec((tm, n), row)],
        out_shape=[jax.ShapeDtypeStruct((m, n), F32), jax.ShapeDtypeStruct((m, n), BF16)],
        compiler_params=_cparams("parallel"),
        name=name,
    )(h, x, g.reshape(1, n), b.reshape(1, n))


def s5_prepare(a_re, a_im, log_dt, b_re, b_im, c_re, c_im, d_skip):
    hi = lax.Precision.HIGHEST
    L = S5_CHUNK
    lam_re, lam_im = a_re.astype(F32), a_im.astype(F32)
    dt = jnp.exp(log_dt.astype(F32))[:, None]
    mag = jnp.exp(lam_re * dt)
    abar_re, abar_im = mag * jnp.cos(lam_im * dt), mag * jnp.sin(lam_im * dt)
    den = lam_re * lam_re + lam_im * lam_im
    num_re = abar_re - 1.0
    f_re = (num_re * lam_re + abar_im * lam_im) / den
    f_im = (abar_im * lam_re - num_re * lam_im) / den
    bre, bim = b_re.astype(F32), b_im.astype(F32)
    bbar_re = f_re[..., None] * bre - f_im[..., None] * bim
    bbar_im = f_re[..., None] * bim + f_im[..., None] * bre
    steps = jnp.arange(L + 1, dtype=F32)[:, None, None]
    pw_mag = jnp.exp(lam_re * dt * steps)
    pw_re = pw_mag * jnp.cos(lam_im * dt * steps)
    pw_im = pw_mag * jnp.sin(lam_im * dt * steps)
    cre, cim = c_re.astype(F32), c_im.astype(F32)
    ca_re = cre[None] * pw_re[:, :, None, :] - cim[None] * pw_im[:, :, None, :]
    ca_im = cre[None] * pw_im[:, :, None, :] + cim[None] * pw_re[:, :, None, :]
    kern = (jnp.einsum('dgpn,gnq->dgpq', ca_re[:L], bbar_re, precision=hi)
            - jnp.einsum('dgpn,gnq->dgpq', ca_im[:L], bbar_im, precision=hi))
    kern = kern.at[0].add(d_skip.astype(F32)[:, :, None] * jnp.eye(S5_GROUP, dtype=F32)[None])
    lag = jnp.arange(L)[None, :] - jnp.arange(L)[:, None]
    t5 = jnp.where((lag >= 0)[:, :, None, None, None], kern[jnp.clip(lag, 0, L - 1)], 0.0)
    g = a_re.shape[0]
    t_mat = t5.transpose(2, 0, 4, 1, 3).reshape(g, L * S5_GROUP, L * S5_GROUP)
    rev_re, rev_im = pw_re[L - 1::-1][:L], pw_im[L - 1::-1][:L]
    bc_re = rev_re[..., None] * bbar_re[None] - rev_im[..., None] * bbar_im[None]
    bc_im = rev_re[..., None] * bbar_im[None] + rev_im[..., None] * bbar_re[None]
    bc_mat = jnp.concatenate([bc_re.transpose(1, 0, 3, 2), bc_im.transpose(1, 0, 3, 2)],
                             axis=-1).reshape(g, L * S5_GROUP, 2 * S5_STATE)
    cc_re = ca_re[1:].transpose(1, 3, 0, 2)
    cc_im = -ca_im[1:].transpose(1, 3, 0, 2)
    cc_mat = jnp.concatenate([cc_re, cc_im], axis=1).reshape(g, 2 * S5_STATE, L * S5_GROUP)
    a1 = jnp.concatenate([pw_re[L], pw_re[L]], axis=-1)
    a2 = jnp.concatenate([-pw_im[L], pw_im[L]], axis=-1)
    return t_mat.astype(BF16), bc_mat.astype(BF16), cc_mat.astype(BF16), a1, a2


def _s5_z_kernel(u_ref, bc_ref, z_ref):
    z_ref[...] = jnp.dot(u_ref[0], bc_ref[0], preferred_element_type=F32)


def _s5_scan_kernel(z_ref, a1_ref, a2_ref, s_ref, carry_ref):
    @pl.when(pl.program_id(0) == 0)
    def _():
        carry_ref[...] = jnp.zeros_like(carry_ref)

    a1 = a1_ref[...]
    a2 = a2_ref[...]
    width = z_ref.shape[1]
    lane = lax.broadcasted_iota(jnp.int32, (1, width), 1)
    real_half = (lane & (2 * S5_STATE - 1)) < S5_STATE

    def body(c, state):
        s_ref[pl.ds(c, 1), :] = state
        swapped = jnp.where(real_half, pltpu.roll(state, width - S5_STATE, axis=1),
                            pltpu.roll(state, S5_STATE, axis=1))
        return a1 * state + a2 * swapped + z_ref[pl.ds(c, 1), :]

    carry_ref[...] = lax.fori_loop(0, z_ref.shape[0], body, carry_ref[...])


def _s5_out_kernel(u_ref, t_ref, s_ref, cc_ref, y_ref):
    y = jnp.dot(u_ref[0], t_ref[0], preferred_element_type=F32)
    y = y + jnp.dot(s_ref[...].astype(BF16), cc_ref[0], preferred_element_type=F32)
    y_ref[0] = _gelu(y)


def s5_branch(u_r, t_mat, bc_mat, cc_mat, a1, a2):
    g, c, w = u_r.shape
    ns = 2 * S5_STATE
    z = pl.pallas_call(
        _s5_z_kernel,
        grid=(g,),
        in_specs=[pl.BlockSpec((1, c, w), lambda i: (i, 0, 0)), pl.BlockSpec((1, w, ns), lambda i: (i, 0, 0))],
        out_specs=pl.BlockSpec((c, ns), lambda i: (0, i)),
        out_shape=jax.ShapeDtypeStruct((c, g * ns), F32),
        compiler_params=_cparams("parallel"),
        name="s5_chunk_inputs",
    )(u_r, bc_mat)
    cb = min(64, c)
    s_prev = pl.pallas_call(
        _s5_scan_kernel,
        grid=(c // cb,),
        in_specs=[pl.BlockSpec((cb, g * ns), lambda i: (i, 0)),
                  pl.BlockSpec((1, g * ns), lambda i: (0, 0)), pl.BlockSpec((1, g * ns), lambda i: (0, 0))],
        out_specs=pl.BlockSpec((cb, g * ns), lambda i: (i, 0)),
        out_shape=jax.ShapeDtypeStruct((c, g * ns), F32),
        scratch_shapes=[pltpu.VMEM((1, g * ns), F32)],
        compiler_params=_cparams("arbitrary"),
        name="s5_chunk_scan",
    )(z, a1.reshape(1, g * ns), a2.reshape(1, g * ns))
    return pl.pallas_call(
        _s5_out_kernel,
        grid=(g,),
        in_specs=[pl.BlockSpec((1, c, w), lambda i: (i, 0, 0)), pl.BlockSpec((1, w, w), lambda i: (i, 0, 0)),
                  pl.BlockSpec((c, ns), lambda i: (0, i)), pl.BlockSpec((1, ns, w), lambda i: (i, 0, 0))],
        out_specs=pl.BlockSpec((1, c, w), lambda i: (i, 0, 0)),
        out_shape=jax.ShapeDtypeStruct((g, c, w), F32),
        compiler_params=_cparams("parallel"),
        name="s5_outputs",
    )(u_r, t_mat, s_prev, cc_mat)


def _glu_kernel(y_ref, w_ref, b_ref, o_ref):
    y = y_ref[...]
    gate = jnp.dot(y.astype(BF16), w_ref[...], preferred_element_type=F32) + b_ref[...]
    o_ref[...] = (y * _sigmoid(gate)).astype(o_ref.dtype)


def glu(y, w, b, tm):
    m, n = y.shape
    return pl.pallas_call(
        _glu_kernel,
        grid=(m // tm,),
        in_specs=[pl.BlockSpec((tm, n), lambda i: (i, 0)), pl.BlockSpec((n, n), lambda i: (0, 0)),
                  pl.BlockSpec((1, n), lambda i: (0, 0))],
        out_specs=pl.BlockSpec((tm, n), lambda i: (i, 0)),
        out_shape=jax.ShapeDtypeStruct((m, n), BF16),
        compiler_params=_cparams("parallel"),
        name="s5_glu",
    )(y, w, b.reshape(1, n))


def _lru_kernel(x_ref, gate_ref, cw_ref, cb_ref, wa_ref, ba_ref, wx_ref, bx_ref, lam_ref, o_ref,
                tail_ref, h_ref, a_ref, b_ref):
    tb, width = x_ref.shape
    blk = width // LRU_BLOCKS

    @pl.when(pl.program_id(0) == 0)
    def _():
        tail_ref[...] = jnp.zeros_like(tail_ref)
        h_ref[...] = jnp.zeros_like(h_ref)

    x = x_ref[...]
    ext = jnp.concatenate([tail_ref[...], x], axis=0)
    tail_ref[...] = x[tb - 8:, :]
    xc = cb_ref[...] + sum(cw_ref[j:j + 1, :] * ext[8 - (CONV_K - 1) + j: 8 - (CONV_K - 1) + j + tb, :]
                           for j in range(CONV_K))
    xcb = xc.astype(BF16)
    r = jnp.concatenate([jnp.dot(xcb[:, i * blk:(i + 1) * blk], wa_ref[i], preferred_element_type=F32)
                         for i in range(LRU_BLOCKS)], axis=1)
    gi = jnp.concatenate([jnp.dot(xcb[:, i * blk:(i + 1) * blk], wx_ref[i], preferred_element_type=F32)
                          for i in range(LRU_BLOCKS)], axis=1)
    r = _sigmoid(r + ba_ref[...])
    gi = _sigmoid(gi + bx_ref[...])
    log_a = -LRU_C * r * _softplus(-lam_ref[...])
    a = jnp.exp(log_a)
    a_ref[...] = a
    b_ref[...] = jnp.sqrt(-jnp.tanh(log_a) * (a * a + 1.0)) * (gi * xc)

    def body(t, h):
        h = a_ref[pl.ds(t, 1), :] * h + b_ref[pl.ds(t, 1), :]
        b_ref[pl.ds(t, 1), :] = h
        return h

    h_ref[...] = lax.fori_loop(0, tb, body, h_ref[...], unroll=8)
    o_ref[...] = (b_ref[...] * _gelu(gate_ref[...])).astype(o_ref.dtype)


def lru_branch(proj, x_col, gate_col, conv_w, conv_b, w_a, b_a, w_x, b_x, lam, tb):
    s = proj.shape[0]
    width = conv_w.shape[1]
    fixed2 = lambda i: (0, 0)
    fixed3 = lambda i: (0, 0, 0)
    vec = lambda v: v.reshape(1, width).astype(F32)
    return pl.pallas_call(
        _lru_kernel,
        grid=(s // tb,),
        in_specs=[pl.BlockSpec((tb, width), lambda i: (i, x_col)), pl.BlockSpec((tb, width), lambda i: (i, gate_col)),
                  pl.BlockSpec((CONV_K, width), fixed2), pl.BlockSpec((1, width), fixed2),
                  pl.BlockSpec(w_a.shape, fixed3), pl.BlockSpec((1, width), fixed2),
                  pl.BlockSpec(w_x.shape, fixed3), pl.BlockSpec((1, width), fixed2),
                  pl.BlockSpec((1, width), fixed2)],
        out_specs=pl.BlockSpec((tb, width), lambda i: (i, 0)),
        out_shape=jax.ShapeDtypeStruct((s, width), BF16),
        scratch_shapes=[pltpu.VMEM((8, width), F32), pltpu.VMEM((1, width), F32),
                        pltpu.VMEM((tb, width), F32), pltpu.VMEM((tb, width), F32)],
        compiler_params=_cparams("arbitrary"),
        name="rglru",
    )(proj, proj, conv_w.astype(F32), vec(conv_b), w_a.astype(BF16), vec(b_a), w_x.astype(BF16), vec(b_x), vec(lam))


def _split3(x):
    hi = x.astype(BF16)
    r1 = x - hi.astype(F32)
    mid = r1.astype(BF16)
    lo = (r1 - mid.astype(F32)).astype(BF16)
    return hi, mid, lo


def _nt(a, b):
    return lax.dot_general(a, b, (((1,), (1,)), ((), ())), preferred_element_type=F32)


def _gdn_kernel(qkv_ref, og_ref, ba_ref, cw_ref, alog_ref, dtb_ref, ng_ref, o_ref,
                tail_ref, state_ref, q_s, k_s, v_s, beta_s, g_s):
    tb = qkv_ref.shape[0]
    c = GDN_CHUNK
    dk, dv = GDN_DK, GDN_DV
    nqk = GDN_HEADS * dk

    @pl.when(pl.program_id(0) == 0)
    def _():
        tail_ref[...] = jnp.zeros_like(tail_ref)
        state_ref[...] = jnp.zeros_like(state_ref)

    x = qkv_ref[...]
    ext = jnp.concatenate([tail_ref[...], x], axis=0)
    tail_ref[...] = x[tb - 8:, :]
    xc = sum(cw_ref[j:j + 1, :] * ext[8 - (CONV_K - 1) + j: 8 - (CONV_K - 1) + j + tb, :] for j in range(CONV_K))
    xc = xc * _sigmoid(xc)
    for h in range(GDN_HEADS):
        qh = xc[:, h * dk:(h + 1) * dk]
        kh = xc[:, nqk + h * dk: nqk + (h + 1) * dk]
        q_s[:, h * dk:(h + 1) * dk] = qh * lax.rsqrt(jnp.sum(qh * qh, axis=-1, keepdims=True) + 1e-6) * dk ** -0.5
        k_s[:, h * dk:(h + 1) * dk] = kh * lax.rsqrt(jnp.sum(kh * kh, axis=-1, keepdims=True) + 1e-6)
    v_s[...] = xc[:, 2 * nqk:]
    ba = ba_ref[...]
    beta_s[...] = _sigmoid(ba)
    g_s[...] = -jnp.exp(alog_ref[...]) * _softplus(ba + dtb_ref[...])

    ri = lax.broadcasted_iota(jnp.int32, (c, c), 0)
    ci = lax.broadcasted_iota(jnp.int32, (c, c), 1)
    causal = ri >= ci
    strict = ri > ci
    tril = jnp.where(causal, 1.0, 0.0).astype(BF16)
    eye = jnp.where(ri == ci, 1.0, 0.0)
    lane = lax.broadcasted_iota(jnp.int32, (c, 128), 1)
    pad_x = jnp.where((lane >= 3) & (lane < 6), 1.0, 0.0)
    pad_y = jnp.where(lane < 3, 1.0, 0.0)

    def chunk(ic, _):
        r0 = pl.multiple_of(ic * c, c)
        rows = pl.ds(r0, c)
        g_parts = _split3(g_s[rows, :])
        gcum = sum(jnp.dot(tril, part, preferred_element_type=F32) for part in g_parts)
        beta = beta_s[rows, :]
        heads = range(GDN_HEADS)
        hs = [slice(h * dk, (h + 1) * dk) for h in heads]
        dot = functools.partial(jnp.dot, preferred_element_type=F32)
        gcol = [gcum[:, GDN_HEADS + h:GDN_HEADS + h + 1] for h in heads]
        q = [q_s[rows, hs[h]] for h in heads]
        k = [k_s[rows, hs[h]] for h in heads]
        kbf = [k[h].astype(BF16) for h in heads]
        kb = [k[h] * beta[:, h:h + 1] for h in heads]
        vb = [(v_s[rows, hs[h]] * beta[:, h:h + 1]).astype(BF16) for h in heads]
        decay = []
        for h in heads:
            hi, mid, lo = (p.astype(F32) for p in _split3(gcol[h]))
            xa = jnp.where(lane == 0, hi, jnp.where(lane == 1, mid, jnp.where(lane == 2, lo, pad_x)))
            ya = jnp.where(lane == 3, -hi, jnp.where(lane == 4, -mid, jnp.where(lane == 5, -lo, pad_y)))
            decay.append(jnp.exp(jnp.where(causal, _nt(xa.astype(BF16), ya.astype(BF16)), -jnp.inf)))
        neg_a = [jnp.where(strict, -_nt(kb[h].astype(BF16), kbf[h]) * decay[h], 0.0) for h in heads]
        intra = [jnp.where(causal, _nt(q[h].astype(BF16), kbf[h]) * decay[h], 0.0).astype(BF16) for h in heads]
        t_off = list(neg_a)
        pw = list(neg_a)
        for _ in range(5):
            pwb = [pw[h].astype(BF16) for h in heads]
            pw = [dot(pwb[h], pwb[h]) for h in heads]
            t_off = [t_off[h] + pw[h] + dot(t_off[h].astype(BF16), pw[h].astype(BF16)) for h in heads]
        t_inv = [(eye + t_off[h]).astype(BF16) for h in heads]
        eg = [jnp.exp(gcol[h]) for h in heads]
        w = [dot(t_inv[h], (kb[h] * eg[h]).astype(BF16)).astype(BF16) for h in heads]
        u = [dot(t_inv[h], vb[h]) for h in heads]
        state = [state_ref[h] for h in heads]
        sb = [state[h].astype(BF16) for h in heads]
        vnb = [(u[h] - dot(w[h], sb[h])).astype(BF16) for h in heads]
        g_last = [gcol[h][c - 1:c, :] for h in heads]
        k_dec = [(k[h] * jnp.exp(g_last[h] - gcol[h])).astype(BF16) for h in heads]
        for h in heads:
            state_ref[h] = state[h] * jnp.exp(g_last[h]) + lax.dot_general(
                k_dec[h], vnb[h], (((0,), (0,)), ((), ())), preferred_element_type=F32)
        o = [dot((q[h] * eg[h]).astype(BF16), sb[h]) + dot(intra[h], vnb[h]) for h in heads]
        for h in heads:
            on = o[h] * lax.rsqrt(jnp.mean(o[h] * o[h], axis=-1, keepdims=True) + 1e-6) * ng_ref[...]
            og = og_ref[rows, hs[h]]
            o_ref[rows, hs[h]] = (on * (og * _sigmoid(og))).astype(o_ref.dtype)
        return 0

    lax.fori_loop(0, tb // c, chunk, 0)


def gdn_branch_pallas(proj, qkv_col, og_col, ba_col, conv_w, a_log, dt_bias, norm_g, tb):
    s = proj.shape[0]
    nv = GDN_HEADS * GDN_DV
    nqkv = conv_w.shape[1]
    pad = jnp.zeros((GDN_HEADS,), F32)
    lanes = lambda v: jnp.concatenate([pad, v.astype(F32), jnp.zeros((128 - 2 * GDN_HEADS,), F32)]).reshape(1, 128)
    fixed = lambda i: (0, 0)
    return pl.pallas_call(
        _gdn_kernel,
        grid=(s // tb,),
        in_specs=[pl.BlockSpec((tb, nqkv), lambda i: (i, qkv_col)), pl.BlockSpec((tb, nv), lambda i: (i, og_col)),
                  pl.BlockSpec((tb, 128), lambda i: (i, ba_col)), pl.BlockSpec((CONV_K, nqkv), fixed),
                  pl.BlockSpec((1, 128), fixed), pl.BlockSpec((1, 128), fixed), pl.BlockSpec((1, GDN_DV), fixed)],
        out_specs=pl.BlockSpec((tb, nv), lambda i: (i, 0)),
        out_shape=jax.ShapeDtypeStruct((s, nv), BF16),
        scratch_shapes=[pltpu.VMEM((8, nqkv), F32), pltpu.VMEM((GDN_HEADS, GDN_DK, GDN_DV), F32),
                        pltpu.VMEM((tb, nv), F32), pltpu.VMEM((tb, nv), F32), pltpu.VMEM((tb, nv), F32),
                        pltpu.VMEM((tb, 128), F32), pltpu.VMEM((tb, 128), F32)],
        compiler_params=_cparams("arbitrary"),
        name="gated_deltanet",
    )(proj, proj, proj, conv_w.astype(F32), lanes(a_log), lanes(dt_bias), norm_g.astype(F32).reshape(1, GDN_DV))


def _merge_kernel(ya_ref, yb_ref, yc_ref, wb_ref, ga_ref, gb_ref, gc_ref, o_ref):
    acc = None
    for n, (y_ref, g_ref) in enumerate(((ya_ref, ga_ref), (yb_ref, gb_ref), (yc_ref, gc_ref))):
        d = jnp.dot(y_ref[...], wb_ref[n], preferred_element_type=F32) * _sigmoid(g_ref[...])
        acc = d if acc is None else acc + d
    o_ref[...] = acc.astype(o_ref.dtype)


def merge_branches(ya, yb, yc, w_branch, proj, gate_col0, tm, tn):
    s, width = ya.shape
    d = w_branch.shape[2]
    nj = d // tn
    c0 = gate_col0 // tn
    yspec = pl.BlockSpec((tm, width), lambda i, j: (i, 0))
    gspec = lambda n: pl.BlockSpec((tm, tn), lambda i, j: (i, c0 + n * nj + j))
    return pl.pallas_call(
        _merge_kernel,
        grid=(s // tm, nj),
        in_specs=[yspec, yspec, yspec, pl.BlockSpec((N_BRANCH, width, tn), lambda i, j: (0, 0, j)),
                  gspec(0), gspec(1), gspec(2)],
        out_specs=pl.BlockSpec((tm, tn), lambda i, j: (i, j)),
        out_shape=jax.ShapeDtypeStruct((s, d), BF16),
        compiler_params=_cparams("parallel", "parallel"),
        name="merge_branches",
    )(ya, yb, yc, w_branch, proj, proj, proj)


def _xattn_kernel(q_ref, k_ref, v_ref, o_ref):
    d = q_ref.shape[1]
    hd = d // XA_HEADS
    outs = []
    for h in range(XA_HEADS):
        sl = slice(h * hd, (h + 1) * hd)
        s = lax.dot_general(q_ref[:, sl], k_ref[:, sl], (((1,), (1,)), ((), ())),
                            preferred_element_type=F32) * hd ** -0.5
        s = s - jnp.max(s, axis=-1, keepdims=True)
        p = jnp.exp(s)
        p = p / jnp.sum(p, axis=-1, keepdims=True)
        outs.append(jnp.dot(p.astype(BF16), v_ref[:, sl], preferred_element_type=F32))
    o_ref[...] = jnp.concatenate(outs, axis=1).astype(o_ref.dtype)


def cross_attention(q, k, v, tq):
    s, d = q.shape
    m = k.shape[0]
    return pl.pallas_call(
        _xattn_kernel,
        grid=(s // tq,),
        in_specs=[pl.BlockSpec((tq, d), lambda i: (i, 0)), pl.BlockSpec((m, d), lambda i: (0, 0)),
                  pl.BlockSpec((m, d), lambda i: (0, 0))],
        out_specs=pl.BlockSpec((tq, d), lambda i: (i, 0)),
        out_shape=jax.ShapeDtypeStruct((s, d), BF16),
        compiler_params=_cparams("parallel"),
        name="cross_attention",
    )(q, k, v)


def _peer_scores_kernel(x_ref, wq_ref, keys_ref, o_ref):
    q = jnp.dot(x_ref[...], wq_ref[...], preferred_element_type=F32).astype(BF16)
    nblk, nk, dk = keys_ref.shape
    for b in range(nblk):
        o_ref[b * nk:(b + 1) * nk, :] = lax.dot_general(
            keys_ref[b], q[:, b * dk:(b + 1) * dk], (((1,), (1,)), ((), ())), preferred_element_type=F32)


def peer_scores(xb, wq, keys, tm):
    s, d = xb.shape
    nblk, nk, dk = keys.shape
    return pl.pallas_call(
        _peer_scores_kernel,
        grid=(s // tm,),
        in_specs=[pl.BlockSpec((tm, d), lambda i: (i, 0)), pl.BlockSpec(wq.shape, lambda i: (0, 0)),
                  pl.BlockSpec(keys.shape, lambda i: (0, 0, 0))],
        out_specs=pl.BlockSpec((nblk * nk, tm), lambda i: (0, i)),
        out_shape=jax.ShapeDtypeStruct((nblk * nk, s), F32),
        compiler_params=_cparams("parallel"),
        name="peer_scores",
    )(xb, wq, keys)


def _select_round(work, rows, sentinel):
    m = jnp.max(work, axis=0, keepdims=True)
    idx = jnp.min(jnp.where(work == m, rows, sentinel), axis=0, keepdims=True)
    hit = rows == idx
    return m, idx, hit, jnp.where(hit, -jnp.inf, work)


def _peer_route_kernel(s_ref, sel0_ref, sel1_ref, pad_ref, grp_ref, e1_ref, n_ref, rk_ref, e2_ref):
    nk, k = PEER_NKEYS, PEER_TOPK
    tt = s_ref.shape[1]
    nc = sel0_ref.shape[0]
    rows = lax.broadcasted_iota(jnp.int32, (nk, tt), 0)
    rows_k = lax.broadcasted_iota(jnp.int32, (k, tt), 0)
    rows_c = lax.broadcasted_iota(jnp.int32, (nc, tt), 0)

    def head(h, _):
        s0 = s_ref[pl.ds(pl.multiple_of(h * 2 * nk, nk), nk), :]
        s1 = s_ref[pl.ds(pl.multiple_of(h * 2 * nk + nk, nk), nk), :]

        def stage1(r, carry):
            w0, ts0, ti0, w1, ts1, rk1 = carry
            m0, i0, _, w0 = _select_round(w0, rows, nk)
            m1, _, hit1, w1 = _select_round(w1, rows, nk)
            cur = rows_k == r
            return (w0, jnp.where(cur, m0, ts0), jnp.where(cur, i0, ti0),
                    w1, jnp.where(cur, m1, ts1), jnp.where(hit1, r, rk1))

        zk = jnp.zeros((k, tt), F32)
        w0, ts0, ti0, w1, ts1, rk1 = lax.fori_loop(
            0, k, stage1, (s0, zk, jnp.zeros((k, tt), jnp.int32), s1, zk, jnp.full((nk, tt), k, jnp.int32)))

        def pick(sel_ref, vals):
            return sum(jnp.dot(sel_ref[...], part, preferred_element_type=F32) for part in _split3(vals))

        cand = pick(sel0_ref, ts0) + pick(sel1_ref, ts1) + pad_ref[...]

        def stage2(r, carry):
            wc, best = carry
            m, _, _, wc = _select_round(wc, rows_c, nc)
            return wc, jnp.where(rows_k == r, m, best)

        wc, best = lax.fori_loop(0, k, stage2, (cand, zk))
        taken = jnp.where(wc == -jnp.inf, 1.0, 0.0).astype(BF16)
        n_rank = jnp.dot(grp_ref[...], taken, preferred_element_type=F32)
        n_key = jnp.zeros((nk, tt), F32)
        for i in range(k):
            n_key = jnp.where(rows == ti0[i:i + 1, :], n_rank[i:i + 1, :], n_key)
        norm = jnp.sum(jnp.exp(best - best[0:1, :]), axis=0, keepdims=True)
        out = pl.ds(pl.multiple_of(h * nk, nk), nk)
        e1_ref[out, :] = jnp.where(w0 == -jnp.inf, jnp.exp(s0 - ts0[0:1, :]), 0.0)
        n_ref[out, :] = n_key
        rk_ref[out, :] = rk1.astype(F32).astype(BF16)
        e2_ref[out, :] = (jnp.where(rk1 < k, jnp.exp(s1 - ts1[0:1, :]), 0.0) / norm).astype(BF16)
        return 0

    lax.fori_loop(0, PEER_HEADS, head, 0)


def peer_route(scores_t, tt):
    r, s = scores_t.shape
    hk = r // 2
    k = PEER_TOPK
    pairs = [(i, j) for i in range(k) for j in range(k // (i + 1))]
    nc = -(-len(pairs) // 8) * 8
    ranks = jnp.arange(k)[None, :]
    first = jnp.array([p[0] for p in pairs] + [-1] * (nc - len(pairs)))[:, None]
    second = jnp.array([p[1] for p in pairs] + [-1] * (nc - len(pairs)))[:, None]
    sel0 = (first == ranks).astype(BF16)
    sel1 = (second == ranks).astype(BF16)
    pad = jnp.where(first >= 0, 0.0, -jnp.inf).astype(F32)
    fixed = lambda i: (0, 0)
    out = pl.BlockSpec((hk, tt), lambda i: (0, i))
    return pl.pallas_call(
        _peer_route_kernel,
        grid=(s // tt,),
        in_specs=[pl.BlockSpec((r, tt), lambda i: (0, i)), pl.BlockSpec((nc, k), fixed), pl.BlockSpec((nc, k), fixed),
                  pl.BlockSpec((nc, 1), fixed), pl.BlockSpec((k, nc), fixed)],
        out_specs=[out, out, out, out],
        out_shape=[jax.ShapeDtypeStruct((hk, s), F32), jax.ShapeDtypeStruct((hk, s), F32),
                   jax.ShapeDtypeStruct((hk, s), BF16), jax.ShapeDtypeStruct((hk, s), BF16)],
        compiler_params=_cparams("parallel"),
        name="peer_route",
    )(scores_t, sel0, sel1, pad, sel0.T)


def _peer_dense_kernel(x_ref, u_ref, vt_ref, e1_ref, n_ref, rk_ref, e2_ref, o_ref, w_s, h_s):
    @pl.when(pl.program_id(1) == 0)
    def _():
        o_ref[...] = jnp.zeros_like(o_ref)

    te = u_ref.shape[0]
    tt = x_ref.shape[0]
    na = te // PEER_NKEYS
    nsub = te // PEER_SUB
    asub = na // nsub

    def gates(sb):
        for ai in range(sb * asub, (sb + 1) * asub):
            w = None
            for hd in range(PEER_HEADS):
                rows = slice(hd * PEER_NKEYS, (hd + 1) * PEER_NKEYS)
                n = jnp.broadcast_to(n_ref[hd, ai:ai + 1, :].astype(BF16), (PEER_NKEYS, tt))
                e1 = jnp.broadcast_to(e1_ref[hd, ai:ai + 1, :].astype(BF16), (PEER_NKEYS, tt))
                term = jnp.where(rk_ref[rows, :] < n, e2_ref[rows, :], jnp.zeros((), BF16)) * e1
                w = term if w is None else w + term
            w_s[ai * PEER_NKEYS:(ai + 1) * PEER_NKEYS, :] = w

    def first(sb):
        rows = slice(sb * PEER_SUB, (sb + 1) * PEER_SUB)
        h_s[rows, :] = lax.dot_general(u_ref[rows, :], x_ref[...], (((1,), (1,)), ((), ())),
                                       preferred_element_type=F32)

    def second(sb):
        rows = slice(sb * PEER_SUB, (sb + 1) * PEER_SUB)
        act = _gelu(h_s[rows, :]).astype(BF16) * w_s[rows, :]
        o_ref[...] += jnp.dot(vt_ref[:, rows], act, preferred_element_type=F32)

    first(0)
    gates(0)
    for sb in range(nsub):
        if sb + 1 < nsub:
            first(sb + 1)
            gates(sb + 1)
        second(sb)


def peer_dense(xb, u_tab, vt_tab, e1t, nt, rkt, e2t, tt, te):
    s, d = xb.shape
    ne = u_tab.shape[0]
    hk = e1t.shape[0]
    side = pl.BlockSpec((hk, tt), lambda i, j: (0, i))
    na = te // PEER_NKEYS
    first = pl.BlockSpec((PEER_HEADS, na, tt), lambda i, j: (0, j, i))
    e1t = e1t.reshape(PEER_HEADS, PEER_NKEYS, s)
    nt = nt.reshape(PEER_HEADS, PEER_NKEYS, s)
    return pl.pallas_call(
        _peer_dense_kernel,
        grid=(s // tt, ne // te),
        in_specs=[pl.BlockSpec((tt, d), lambda i, j: (i, 0)), pl.BlockSpec((te, d), lambda i, j: (j, 0)),
                  pl.BlockSpec((d, te), lambda i, j: (0, j)), first, first, side, side],
        out_specs=pl.BlockSpec((d, tt), lambda i, j: (0, i)),
        out_shape=jax.ShapeDtypeStruct((d, s), F32),
        scratch_shapes=[pltpu.VMEM((te, tt), BF16), pltpu.VMEM((te, tt), F32)],
        compiler_params=_cparams("parallel", "arbitrary"),
        name="peer_dense",
    )(xb, u_tab, vt_tab, e1t, nt, rkt, e2t)


def _res_ln_t_kernel(ht_ref, x_ref, g_ref, b_ref, o_ref, ob_ref):
    y = _layer_norm(DN_ALPHA * x_ref[...] + ht_ref[...].T, g_ref[...], b_ref[...])
    o_ref[...] = y
    ob_ref[...] = y.astype(BF16)


def res_ln_t(ht, x, g, b, tm, name):
    m, n = x.shape
    row = lambda i: (i, 0)
    fixed = lambda i: (0, 0)
    return pl.pallas_call(
        _res_ln_t_kernel,
        grid=(m // tm,),
        in_specs=[pl.BlockSpec((n, tm), lambda i: (0, i)), pl.BlockSpec((tm, n), row),
                  pl.BlockSpec((1, n), fixed), pl.BlockSpec((1, n), fixed)],
        out_specs=[pl.BlockSpec((tm, n), row), pl.BlockSpec((tm, n), row)],
        out_shape=[jax.ShapeDtypeStruct((m, n), F32), jax.ShapeDtypeStruct((m, n), BF16)],
        compiler_params=_cparams("parallel"),
        name=name,
    )(ht, x, g.reshape(1, n), b.reshape(1, n))


def _layer(x, xb, memb, p):
    s, d = x.shape
    half = d // 2
    w_in = p['w_in']
    n_main = 7 * half
    n_small = 2 * GDN_HEADS
    n_used = w_in.shape[1] - half
    n_proj = -(-n_used // IN_PROJ_TN) * IN_PROJ_TN
    w_cat = jnp.concatenate([w_in[:, 3 * half:6 * half], w_in[:, half:3 * half], w_in[:, 6 * half:n_main],
                             w_in[:, n_main + n_small:], w_in[:, n_main:n_main + n_small],
                             jnp.zeros((d, n_proj - n_used), w_in.dtype)], axis=1).astype(BF16)
    proj = mm(xb, w_cat, F32, 1024, IN_PROJ_TN, "in_proj")
    qkv_blk, lru_x_blk, lru_gate_blk, og_blk = 0, 3, 4, 5
    gate_col0 = 6 * half
    small0 = gate_col0 + N_BRANCH * d

    g_s5 = half // S5_GROUP
    nchunk = s // S5_CHUNK
    u_s5 = mm(xb, w_in[:, :half].astype(BF16), BF16, 1024, IN_PROJ_TN, "s5_in_proj")
    u_r = (u_s5.reshape(nchunk, S5_CHUNK, g_s5, S5_GROUP)
           .transpose(2, 0, 1, 3).reshape(g_s5, nchunk, S5_CHUNK * S5_GROUP))
    prep = s5_prepare(p['s5_a_re'], p['s5_a_im'], p['s5_log_dt'], p['s5_b_re'], p['s5_b_im'],
                      p['s5_c_re'], p['s5_c_im'], p['s5_d'])
    y_r = s5_branch(u_r, *prep)
    y_s5 = (y_r.reshape(g_s5, nchunk, S5_CHUNK, S5_GROUP).transpose(1, 2, 0, 3).reshape(s, half))
    y_a = glu(y_s5, p['s5_w_glu'].astype(BF16), p['s5_b_glu'].astype(F32), 512)

    y_b = lru_branch(proj, lru_x_blk, lru_gate_blk, p['lru_conv_w'], p['lru_conv_b'], p['lru_w_a'], p['lru_b_a'],
                     p['lru_w_x'], p['lru_b_x'], p['lru_lambda'], 256)

    y_c = gdn_branch_pallas(proj, qkv_blk, og_blk, small0 // 128, p['gdn_conv_w'], p['gdn_a_log'],
                            p['gdn_dt_bias'], p['gdn_norm_g'], 256)

    merged = merge_branches(y_a, y_b, y_c, p['w_branch'].astype(BF16), proj, gate_col0, 1024, 512)
    x, xb = mm_res_ln(merged, p['w_out'].astype(BF16), x, p['ln_mix_g'], p['ln_mix_b'], 512, "out_proj_ln")

    q = mm(xb, p['xa_wq'].astype(BF16), BF16, 1024, 512, "xa_q")
    k = mm(memb, p['xa_wk'].astype(BF16), BF16, memb.shape[0], 512, "xa_k")
    v = mm(memb, p['xa_wv'].astype(BF16), BF16, memb.shape[0], 512, "xa_v")
    o = cross_attention(q, k, v, 512)
    x, xb = mm_res_ln(o, p['xa_wo'].astype(BF16), x, p['ln_xa_g'], p['ln_xa_b'], 512, "xa_out_ln")

    keys = p['peer_keys']
    keys = keys.reshape(PEER_HEADS * 2, PEER_NKEYS, keys.shape[-1]).astype(BF16)
    scores_t = peer_scores(xb, p['peer_wq'].astype(BF16), keys, 512)
    e1, n, rk, e2 = peer_route(scores_t, 128)
    ht = peer_dense(xb, p['peer_u'].astype(BF16), p['peer_v'].T.astype(BF16), e1, n, rk, e2, 512, 1024)
    return res_ln_t(ht, x, p['ln_ffn_g'], p['ln_ffn_b'], 512, "ffn_ln")


def kernel(x, mem, ln_mix_g, ln_mix_b, w_in, s5_a_re, s5_a_im, s5_log_dt, s5_b_re, s5_b_im, s5_c_re, s5_c_im, s5_d, s5_w_glu, s5_b_glu, lru_conv_w, lru_conv_b, lru_w_a, lru_b_a, lru_w_x, lru_b_x, lru_lambda, gdn_conv_w, gdn_a_log, gdn_dt_bias, gdn_norm_g, w_branch, w_out, ln_xa_g, ln_xa_b, xa_wq, xa_wk, xa_wv, xa_wo, ln_ffn_g, ln_ffn_b, peer_wq, peer_keys, peer_u, peer_v):
    params = dict(ln_mix_g=ln_mix_g, ln_mix_b=ln_mix_b, w_in=w_in, s5_a_re=s5_a_re, s5_a_im=s5_a_im,
                  s5_log_dt=s5_log_dt, s5_b_re=s5_b_re, s5_b_im=s5_b_im, s5_c_re=s5_c_re, s5_c_im=s5_c_im,
                  s5_d=s5_d, s5_w_glu=s5_w_glu, s5_b_glu=s5_b_glu, lru_conv_w=lru_conv_w, lru_conv_b=lru_conv_b,
                  lru_w_a=lru_w_a, lru_b_a=lru_b_a, lru_w_x=lru_w_x, lru_b_x=lru_b_x, lru_lambda=lru_lambda,
                  gdn_conv_w=gdn_conv_w, gdn_a_log=gdn_a_log, gdn_dt_bias=gdn_dt_bias, gdn_norm_g=gdn_norm_g,
                  w_branch=w_branch, w_out=w_out, ln_xa_g=ln_xa_g, ln_xa_b=ln_xa_b, xa_wq=xa_wq, xa_wk=xa_wk,
                  xa_wv=xa_wv, xa_wo=xa_wo, ln_ffn_g=ln_ffn_g, ln_ffn_b=ln_ffn_b, peer_wq=peer_wq,
                  peer_keys=peer_keys, peer_u=peer_u, peer_v=peer_v)
    bsz, seq, d = x.shape
    outs = []
    for b in range(bsz):
        xf = x[b].astype(F32)
        xb = xf.astype(BF16)
        memb = mem[b].astype(BF16)
        for l in range(DEPTH):
            xf, xb = _layer(xf, xb, memb, {k: v[l] for k, v in params.items()})
        outs.append(xf)
    return jnp.stack(outs, axis=0)
```
